```python
import math
import jax, jax.numpy as jnp
from jax import lax
import numpy as np

D_MODEL = 1024
BATCH = 4
SEQ = 8192
DEPTH = 4

N_MIXERS = 2
N_ATTN_LAYERS = (DEPTH + 1) // 2
N_GMLP_LAYERS = DEPTH // 2
D_FF = 2816
DIFF_HEADS = 8
DIFF_HEAD_DIM = D_MODEL // DIFF_HEADS // 2
DIFF_V_DIM = 2 * DIFF_HEAD_DIM
DIFF_QK_WIDTH = DIFF_HEADS * 2 * DIFF_HEAD_DIM
DIFF_V_WIDTH = DIFF_HEADS * DIFF_V_DIM
Q_BLOCK = 128
GMLP_HALF = 2 * D_MODEL
GMLP_GROUPS = 8
GMLP_CHUNK = 128
RMS_EPS = 1e-6
LN_EPS = 1e-5

kernel_name = "hybrid_diffattn_chunked_sgu_macaron"


def _rmsnorm(x, g):
    xf = x.astype(jnp.float32)
    y = xf * lax.rsqrt(jnp.mean(xf * xf, axis=-1, keepdims=True) + RMS_EPS)
    return (y * g.astype(jnp.float32)).astype(x.dtype)


def _layernorm(x, g, b):
    xf = x.astype(jnp.float32)
    mu = jnp.mean(xf, axis=-1, keepdims=True)
    xc = xf - mu
    var = jnp.mean(xc * xc, axis=-1, keepdims=True)
    y = xc * lax.rsqrt(var + LN_EPS) * g.astype(jnp.float32) + b.astype(jnp.float32)
    return y.astype(x.dtype)


def _swiglu_ffn(x, w_gate_up, w_down):
    g, u = jnp.split(x @ w_gate_up, 2, axis=-1)
    return (jax.nn.silu(g) * u) @ w_down


def _lambda_init(layer_idx):
    return 0.8 - 0.6 * math.exp(-0.3 * layer_idx)


def _diff_attention(h, w_in, w_out, q_norm, k_norm, lq1, lk1, lq2, lk2, subln, lam_init):
    B, S, _ = h.shape
    H, d, dv = DIFF_HEADS, DIFF_HEAD_DIM, DIFF_V_DIM
    q, k, v = jnp.split(h @ w_in, [DIFF_QK_WIDTH, 2 * DIFF_QK_WIDTH], axis=-1)
    q = _rmsnorm(q.reshape(B, S, H, 2, d), q_norm) * (d ** -0.5)
    k = _rmsnorm(k.reshape(B, S, H, 2, d), k_norm)
    v = v.reshape(B, S, H, dv)
    lam = (jnp.exp(jnp.sum(lq1.astype(jnp.float32) * lk1.astype(jnp.float32)))
           - jnp.exp(jnp.sum(lq2.astype(jnp.float32) * lk2.astype(jnp.float32)))
           + lam_init)
    outs = []
    for i in range(S // Q_BLOCK):
        L = (i + 1) * Q_BLOCK
        q_blk = q[:, i * Q_BLOCK:L]
        s = jnp.einsum('bqhcd,bkhcd->bhcqk', q_blk, k[:, :L]).astype(jnp.float32)
        qpos = i * Q_BLOCK + jnp.arange(Q_BLOCK)
        kpos = jnp.arange(L)
        mask = kpos[None, :] <= qpos[:, None]
        p = jax.nn.softmax(jnp.where(mask, s, -jnp.inf), axis=-1)
        a = (p[:, :, 0] - lam * p[:, :, 1]).astype(v.dtype)
        outs.append(jnp.einsum('bhqk,bkhe->bqhe', a, v[:, :L]))
    o = jnp.concatenate(outs, axis=1)
    o = _rmsnorm(o, subln) * (1.0 - lam_init)
    return o.reshape(B, S, DIFF_V_WIDTH) @ w_out


def _chunked_sgu(h, w_in, b_in, ln_g, ln_b, w_s, b_s, w_out, b_out):
    B, S, _ = h.shape
    z = jax.nn.gelu(h @ w_in + b_in, approximate=False)
    u, v = jnp.split(z, 2, axis=-1)
    v = _layernorm(v, ln_g, ln_b)
    nc = S // GMLP_CHUNK
    gc = GMLP_HALF // GMLP_GROUPS
    v = v.reshape(B, nc, GMLP_CHUNK, GMLP_GROUPS, gc)
    causal = jnp.tril(jnp.ones((GMLP_CHUNK, GMLP_CHUNK), dtype=bool))
    w = jnp.where(causal, w_s, 0.0)
    s = jnp.einsum('gts,bnsgc->bntgc', w, v) + jnp.transpose(b_s)[None, None, :, :, None]
    gated = u * s.reshape(B, S, GMLP_HALF)
    return gated @ w_out + b_out


def setup_inputs(seed: int = 0) -> dict:
    key = jax.random.key(seed)
    ks = jax.random.split(key, 32)
    f32 = jnp.float32
    nrm = lambda k, shape, scale: jax.random.normal(k, shape, f32) * scale
    gain = lambda k, shape: 1.0 + 0.02 * jax.random.normal(k, shape, f32)
    NA, NG = N_ATTN_LAYERS, N_GMLP_LAYERS
    return {
        "x": jax.random.normal(ks[0], (BATCH, SEQ, D_MODEL), f32),
        "ffn1_norm": gain(ks[1], (DEPTH, D_MODEL)),
        "ffn1_w_gate_up": nrm(ks[2], (DEPTH, D_MODEL, 2 * D_FF), D_MODEL ** -0.5),
        "ffn1_w_down": nrm(ks[3], (DEPTH, D_FF, D_MODEL), D_FF ** -0.5),
        "mix_norm": gain(ks[4], (DEPTH, D_MODEL)),
        "ffn2_norm": gain(ks[5], (DEPTH, D_MODEL)),
        "ffn2_w_gate_up": nrm(ks[6], (DEPTH, D_MODEL, 2 * D_FF), D_MODEL ** -0.5),
        "ffn2_w_down": nrm(ks[7], (DEPTH, D_FF, D_MODEL), D_FF ** -0.5),
        "attn_w_in": nrm(ks[8], (NA, D_MODEL, 2 * DIFF_QK_WIDTH + DIFF_V_WIDTH), D_MODEL ** -0.5),
        "attn_w_out": nrm(ks[9], (NA, DIFF_V_WIDTH, D_MODEL), DIFF_V_WIDTH ** -0.5),
        "attn_q_norm": gain(ks[10], (NA, DIFF_HEAD_DIM)),
        "attn_k_norm": gain(ks[11], (NA, DIFF_HEAD_DIM)),
        "attn_lambda_q1": nrm(ks[12], (NA, DIFF_HEAD_DIM), 0.1),
        "attn_lambda_k1": nrm(ks[13], (NA, DIFF_HEAD_DIM), 0.1),
        "attn_lambda_q2": nrm(ks[14], (NA, DIFF_HEAD_DIM), 0.1),
        "attn_lambda_k2": nrm(ks[15], (NA, DIFF_HEAD_DIM), 0.1),
        "attn_subln": gain(ks[16], (NA, DIFF_V_DIM)),
        "gmlp_w_in": nrm(ks[17], (NG, D_MODEL, 2 * GMLP_HALF), D_MODEL ** -0.5),
        "gmlp_b_in": nrm(ks[18], (NG, 2 * GMLP_HALF), 0.02),
        "gmlp_ln_g": gain(ks[19], (NG, GMLP_HALF)),
        "gmlp_ln_b": nrm(ks[20], (NG, GMLP_HALF), 0.02),
        "gmlp_w_s": nrm(ks[21], (NG, GMLP_GROUPS, GMLP_CHUNK, GMLP_CHUNK), 0.5 * GMLP_CHUNK ** -0.5),
        "gmlp_b_s": gain(ks[22], (NG, GMLP_GROUPS, GMLP_CHUNK)),
        "gmlp_w_out": nrm(ks[23], (NG, GMLP_HALF, D_MODEL), GMLP_HALF ** -0.5),
        "gmlp_b_out": nrm(ks[24], (NG, D_MODEL), 0.02),
    }


def reference(x, ffn1_norm, ffn1_w_gate_up, ffn1_w_down, mix_norm, ffn2_norm, ffn2_w_gate_up,
              ffn2_w_down, attn_w_in, attn_w_out, attn_q_norm, attn_k_norm, attn_lambda_q1,
              attn_lambda_k1, attn_lambda_q2, attn_lambda_k2, attn_subln, gmlp_w_in, gmlp_b_in,
              gmlp_ln_g, gmlp_ln_b, gmlp_w_s, gmlp_b_s, gmlp_w_out, gmlp_b_out):
    for i in range(DEPTH):
        x = x + 0.5 * _swiglu_ffn(_rmsnorm(x, ffn1_norm[i]), ffn1_w_gate_up[i], ffn1_w_down[i])
        h = _rmsnorm(x, mix_norm[i])
        j = i // N_MIXERS
        if i % N_MIXERS == 0:
            x = x + _diff_attention(h, attn_w_in[j], attn_w_out[j], attn_q_norm[j], attn_k_norm[j],
                                    attn_lambda_q1[j], attn_lambda_k1[j], attn_lambda_q2[j],
                                    attn_lambda_k2[j], attn_subln[j], _lambda_init(i))
        else:
            x = x + _chunked_sgu(h, gmlp_w_in[j], gmlp_b_in[j], gmlp_ln_g[j], gmlp_ln_b[j],
                                 gmlp_w_s[j], gmlp_b_s[j], gmlp_w_out[j], gmlp_b_out[j])
        x = x + 0.5 * _swiglu_ffn(_rmsnorm(x, ffn2_norm[i]), ffn2_w_gate_up[i], ffn2_w_down[i])
    return x
```

```python
import functools
import math

import jax
import jax.numpy as jnp
from jax import lax
from jax.experimental import pallas as pl
from jax.experimental.pallas import tpu as pltpu

F32 = jnp.float32
BF16 = jnp.bfloat16

RMS_EPS = 1e-6
LN_EPS = 1e-5
DIFF_HEADS = 8
GMLP_GROUPS = 8
GMLP_CHUNK = 128

MXU_TILE = 256
VMEM_LIMIT_BYTES = 56 * 1024 * 1024
NEG_BIG = -1e30

TOKEN_TILE = 512
ATTN_TILE = 512


def _const_spec(shape):
    return pl.BlockSpec(shape, lambda *_: (0,) * len(shape), pipeline_mode=pl.Buffered(1))


def _params(n_axes):
    return pltpu.CompilerParams(
        dimension_semantics=("arbitrary",) * n_axes,
        vmem_limit_bytes=VMEM_LIMIT_BYTES,
    )


def _rms_rows(x, gain):
    ms = jnp.mean(x * x, axis=-1, keepdims=True)
    return x * lax.rsqrt(ms + RMS_EPS) * gain


def _ffn_kernel(x_ref, g_ref, wgu_ref, wd_ref, o_ref, h_ref, *, d_ff):
    xn = _rms_rows(x_ref[...], g_ref[...]).astype(BF16)
    for j in range(d_ff // MXU_TILE):
        lo = j * MXU_TILE
        gate = jnp.dot(xn, wgu_ref[:, lo:lo + MXU_TILE], preferred_element_type=F32)
        up = jnp.dot(xn, wgu_ref[:, d_ff + lo:d_ff + lo + MXU_TILE], preferred_element_type=F32)
        h_ref[:, lo:lo + MXU_TILE] = (gate * jax.nn.sigmoid(gate) * up).astype(BF16)
    y = jnp.dot(h_ref[...], wd_ref[...], preferred_element_type=F32)
    o_ref[...] = x_ref[...] + 0.5 * y


def _ffn(x, gain, w_gate_up, w_down):
    t, d = x.shape
    d_ff = w_down.shape[0]
    assert d_ff % MXU_TILE == 0 and t % TOKEN_TILE == 0
    tm = TOKEN_TILE
    return pl.pallas_call(
        functools.partial(_ffn_kernel, d_ff=d_ff),
        out_shape=jax.ShapeDtypeStruct((t, d), F32),
        grid=(t // tm,),
        in_specs=[
            pl.BlockSpec((tm, d), lambda i: (i, 0)),
            _const_spec((1, d)),
            _const_spec((d, 2 * d_ff)),
            _const_spec((d_ff, d)),
        ],
        out_specs=pl.BlockSpec((tm, d), lambda i: (i, 0)),
        scratch_shapes=[pltpu.VMEM((tm, d_ff), BF16)],
        compiler_params=_params(1),
        name="ffn",
    )(x, gain, w_gate_up, w_down)


def _group_ones(n, group):
    r = lax.broadcasted_iota(jnp.int32, (n, n), 0) // group
    c = lax.broadcasted_iota(jnp.int32, (n, n), 1) // group
    return (r == c).astype(BF16)


def _attn_in_kernel(x_ref, g_ref, wk_ref, wqt_ref, wvt_ref, kgain_ref, k_ref, qt_ref, vt_ref, *, hd):
    d = x_ref.shape[1]
    xn = _rms_rows(x_ref[...], g_ref[...]).astype(BF16)
    ones = _group_ones(MXU_TILE, hd)
    nt_dims = (((1,), (1,)), ((), ()))

    k = jnp.dot(xn, wk_ref[...], preferred_element_type=F32)
    for j in range(d // MXU_TILE):
        lo = j * MXU_TILE
        kj = k[:, lo:lo + MXU_TILE]
        ss = jnp.dot((kj * kj).astype(BF16), ones, preferred_element_type=F32)
        kn = kj * lax.rsqrt(ss * (1.0 / hd) + RMS_EPS) * kgain_ref[:, lo:lo + MXU_TILE]
        k_ref[:, lo:lo + MXU_TILE] = kn.astype(BF16)

    qt = lax.dot_general(wqt_ref[...], xn, nt_dims, preferred_element_type=F32)
    for j in range(d // MXU_TILE):
        lo = j * MXU_TILE
        qj = qt[lo:lo + MXU_TILE, :]
        ss = jnp.dot(ones, (qj * qj).astype(BF16), preferred_element_type=F32)
        qt_ref[0, lo:lo + MXU_TILE, :] = (qj * lax.rsqrt(ss * (1.0 / hd) + RMS_EPS)).astype(BF16)

    vt = lax.dot_general(wvt_ref[...], xn, nt_dims, preferred_element_type=F32)
    vt_ref[0] = vt.astype(BF16)


def _attn_in(x, gain, wk, wqt, wvt, kgain, hd):
    t, d = x.shape
    tm = ATTN_TILE
    nt = t // tm
    return pl.pallas_call(
        functools.partial(_attn_in_kernel, hd=hd),
        out_shape=(
            jax.ShapeDtypeStruct((t, d), BF16),
            jax.ShapeDtypeStruct((nt, d, tm), BF16),
            jax.ShapeDtypeStruct((nt, d, tm), BF16),
        ),
        grid=(nt,),
        in_specs=[
            pl.BlockSpec((tm, d), lambda i: (i, 0)),
            _const_spec((1, d)),
            _const_spec((d, d)),
            _const_spec((d, d)),
            _const_spec((d, d)),
            _const_spec((1, d)),
        ],
        out_specs=(
            pl.BlockSpec((tm, d), lambda i: (i, 0)),
            pl.BlockSpec((1, d, tm), lambda i: (i, 0, 0)),
            pl.BlockSpec((1, d, tm), lambda i: (i, 0, 0)),
        ),
        compiler_params=_params(1),
        name="attn_in",
    )(x, gain, wk, wqt, wvt, kgain)


def _flash_kernel(qt_ref, k_ref, vt_ref, lq1_ref, lk1_ref, lq2_ref, lk2_ref, sub_ref, o_ref,
                  q_scr, m_scr, l_scr, acc_scr, *, hd, lam_init):
    i = pl.program_id(2)
    tq = qt_ref.shape[3]
    tk = vt_ref.shape[3]

    qt = qt_ref[0, 0]
    row = lax.broadcasted_iota(jnp.int32, qt.shape, 0)
    zero = jnp.zeros_like(qt)
    q_scr[0] = jnp.where(row < hd, qt, zero)
    q_scr[1] = jnp.where(row >= hd, qt, zero)
    m_scr[...] = jnp.full(m_scr.shape, NEG_BIG, F32)
    l_scr[...] = jnp.zeros(l_scr.shape, F32)
    acc_scr[...] = jnp.zeros(acc_scr.shape, F32)

    def kv_block(j, diagonal):
        start = pl.multiple_of(j * tk, tk)
        kj = k_ref[0, pl.ds(start, tk), :]
        vj = vt_ref[0, j]
        for c in range(2):
            s = jnp.dot(kj, q_scr[c], preferred_element_type=F32)
            if diagonal:
                kpos = lax.broadcasted_iota(jnp.int32, s.shape, 0)
                qpos = lax.broadcasted_iota(jnp.int32, s.shape, 1)
                s = jnp.where(kpos <= qpos, s, NEG_BIG)
            m_old = m_scr[c]
            m_new = jnp.maximum(m_old, jnp.max(s, axis=0, keepdims=True))
            alpha = jnp.exp(m_old - m_new)
            p = jnp.exp(s - m_new)
            l_scr[c] = alpha * l_scr[c] + jnp.sum(p, axis=0, keepdims=True)
            acc_scr[c] = alpha * acc_scr[c] + jnp.dot(vj, p.astype(BF16), preferred_element_type=F32)
            m_scr[c] = m_new

    def body(j, carry):
        kv_block(j, diagonal=False)
        return carry

    lax.fori_loop(0, i, body, 0)
    kv_block(i, diagonal=True)

    lam = (jnp.exp(jnp.sum(lq1_ref[...] * lk1_ref[...], axis=-1, keepdims=True))
           - jnp.exp(jnp.sum(lq2_ref[...] * lk2_ref[...], axis=-1, keepdims=True))
           + lam_init)
    o = acc_scr[0] * (1.0 / l_scr[0]) - acc_scr[1] * (lam / l_scr[1])
    ms = jnp.mean(o * o, axis=0, keepdims=True)
    on = o * lax.rsqrt(ms + RMS_EPS) * (sub_ref[...] * (1.0 - lam_init))
    o_ref[0] = on.T.astype(BF16)


def _flash(qt, k, vt, lq1, lk1, lq2, lk2, subln_col, lam_init):
    b, nq, d, tq = qt.shape
    _, s, _ = k.shape
    nk, tk = vt.shape[1], vt.shape[3]
    assert tq == tk and nq == nk
    h = DIFF_HEADS
    hw = d // h
    hd = hw // 2
    vec = pl.BlockSpec((1, hd), lambda bb, hh, ii: (0, 0))
    return pl.pallas_call(
        functools.partial(_flash_kernel, hd=hd, lam_init=lam_init),
        out_shape=jax.ShapeDtypeStruct((b, s, d), BF16),
        grid=(b, h, nq),
        in_specs=[
            pl.BlockSpec((1, 1, hw, tq), lambda bb, hh, ii: (bb, ii, hh, 0)),
            pl.BlockSpec((1, s, hw), lambda bb, hh, ii: (bb, 0, hh)),
            pl.BlockSpec((1, nk, hw, tk), lambda bb, hh, ii: (bb, 0, hh, 0)),
            vec, vec, vec, vec,
            pl.BlockSpec((hw, 1), lambda bb, hh, ii: (0, 0)),
        ],
        out_specs=pl.BlockSpec((1, tq, hw), lambda bb, hh, ii: (bb, ii, hh)),
        scratch_shapes=[
            pltpu.VMEM((2, hw, tq), BF16),
            pltpu.VMEM((2, 1, tq), F32),
            pltpu.VMEM((2, 1, tq), F32),
            pltpu.VMEM((2, hw, tq), F32),
        ],
        compiler_params=_params(3),
        name="flash",
    )(qt, k, vt, lq1, lk1, lq2, lk2, subln_col)


def _attn_out_kernel(x_ref, o_ref, w_ref, y_ref):
    y_ref[...] = x_ref[...] + jnp.dot(o_ref[...], w_ref[...], preferred_element_type=F32)


def _attn_out(x, o, w_out):
    t, d = x.shape
    tm = TOKEN_TILE
    return pl.pallas_call(
        _attn_out_kernel,
        out_shape=jax.ShapeDtypeStruct((t, d), F32),
        grid=(t // tm,),
        in_specs=[
            pl.BlockSpec((tm, d), lambda i: (i, 0)),
            pl.BlockSpec((tm, d), lambda i: (i, 0)),
            _const_spec((d, d)),
        ],
        out_specs=pl.BlockSpec((tm, d), lambda i: (i, 0)),
        compiler_params=_params(1),
        name="attn_out",
    )(x, o, w_out)


def _gelu(z):
    return 0.5 * z * (1.0 + lax.erf(z * math.sqrt(0.5)))


def _gmlp_kernel(x_ref, g_ref, win_ref, bin_ref, lng_ref, lnb_ref, ws_ref, bs_ref, wout_ref, bout_ref,
                 y_ref, v_scr, gated_scr, *, half):
    tm = x_ref.shape[0]
    gw = half // GMLP_GROUPS
    xn = _rms_rows(x_ref[...], g_ref[...]).astype(BF16)

    rsum = jnp.zeros((tm, 1), F32)
    rsq = jnp.zeros((tm, 1), F32)
    for g in range(GMLP_GROUPS):
        lo = half + g * gw
        z = jnp.dot(xn, win_ref[:, lo:lo + gw], preferred_element_type=F32) + bin_ref[:, lo:lo + gw]
        v = _gelu(z)
        v_scr[:, g * gw:(g + 1) * gw] = v
        rsum = rsum + jnp.sum(v, axis=-1, keepdims=True)
        rsq = rsq + jnp.sum(v * v, axis=-1, keepdims=True)
    mu = rsum * (1.0 / half)
    var = rsq * (1.0 / half) - mu * mu
    rstd = lax.rsqrt(var + LN_EPS)

    t_idx = lax.broadcasted_iota(jnp.int32, (GMLP_CHUNK, GMLP_CHUNK), 0)
    s_idx = lax.broadcasted_iota(jnp.int32, (GMLP_CHUNK, GMLP_CHUNK), 1)
    causal = s_idx <= t_idx

    for g in range(GMLP_GROUPS):
        lo = g * gw
        vn = ((v_scr[:, lo:lo + gw] - mu) * rstd * lng_ref[:, lo:lo + gw] + lnb_ref[:, lo:lo + gw]).astype(BF16)
        w = jnp.where(causal, ws_ref[g], 0.0).astype(BF16)
        z = jnp.dot(xn, win_ref[:, lo:lo + gw], preferred_element_type=F32) + bin_ref[:, lo:lo + gw]
        u = _gelu(z)
        for c in range(tm // GMLP_CHUNK):
            r = c * GMLP_CHUNK
            s = jnp.dot(w, vn[r:r + GMLP_CHUNK, :], preferred_element_type=F32) + bs_ref[:, lo:lo + gw]
            gated_scr[r:r + GMLP_CHUNK, lo:lo + gw] = (u[r:r + GMLP_CHUNK, :] * s).astype(BF16)

    y = jnp.dot(gated_scr[...], wout_ref[...], preferred_element_type=F32)
    y_ref[...] = x_ref[...] + y + bout_ref[...]


def _gmlp(x, gain, w_in, b_in, ln_g, ln_b, w_s, bs_full, w_out, b_out):
    t, d = x.shape
    half = w_out.shape[0]
    tm = TOKEN_TILE
    assert tm % GMLP_CHUNK == 0
    return pl.pallas_call(
        functools.partial(_gmlp_kernel, half=half),
        out_shape=jax.ShapeDtypeStruct((t, d), F32),
        grid=(t // tm,),
        in_specs=[
            pl.BlockSpec((tm, d), lambda i: (i, 0)),
            _const_spec((1, d)),
            _const_spec((d, 2 * half)),
            _const_spec((1, 2 * half)),
            _const_spec((1, half)),
            _const_spec((1, half)),
            _const_spec((GMLP_GROUPS, GMLP_CHUNK, GMLP_CHUNK)),
            _const_spec((GMLP_CHUNK, half)),
            _const_spec((half, d)),
            _const_spec((1, d)),
        ],
        out_specs=pl.BlockSpec((tm, d), lambda i: (i, 0)),
        scratch_shapes=[pltpu.VMEM((tm, half), F32), pltpu.VMEM((tm, half), BF16)],
        compiler_params=_params(1),
        name="gmlp",
    )(x, gain, w_in, b_in, ln_g, ln_b, w_s, bs_full, w_out, b_out)


def _lambda_init(layer_idx):
    return 0.8 - 0.6 * math.exp(-0.3 * layer_idx)


def _row(v):
    return v.reshape(1, -1).astype(F32)


def kernel(x, ffn1_norm, ffn1_w_gate_up, ffn1_w_down, mix_norm, ffn2_norm, ffn2_w_gate_up, ffn2_w_down, attn_w_in, attn_w_out, attn_q_norm, attn_k_norm, attn_lambda_q1, attn_lambda_k1, attn_lambda_q2, attn_lambda_k2, attn_subln, gmlp_w_in, gmlp_b_in, gmlp_ln_g, gmlp_ln_b, gmlp_w_s, gmlp_b_s, gmlp_w_out, gmlp_b_out):
    b, s, d = x.shape
    depth = ffn1_norm.shape[0]
    h = DIFF_HEADS
    hd = d // h // 2
    qk_width = h * 2 * hd
    nq = s // ATTN_TILE
    xf = x.reshape(b * s, d)

    for i in range(depth):
        xf = _ffn(xf, _row(ffn1_norm[i]), ffn1_w_gate_up[i].astype(BF16), ffn1_w_down[i].astype(BF16))
        j = i // 2
        if i % 2 == 0:
            w_in = attn_w_in[j].astype(BF16)
            wqt = w_in[:, :qk_width].T
            wk = w_in[:, qk_width:2 * qk_width]
            wvt = w_in[:, 2 * qk_width:].T
            kgain = jnp.tile(attn_q_norm[j] * attn_k_norm[j] * (hd ** -0.5), 2 * h).reshape(1, -1).astype(F32)
            k, qt, vt = _attn_in(xf, _row(mix_norm[i]), wk, wqt, wvt, kgain, hd)
            o = _flash(qt.reshape(b, nq, d, ATTN_TILE), k.reshape(b, s, d), vt.reshape(b, nq, d, ATTN_TILE),
                       _row(attn_lambda_q1[j]), _row(attn_lambda_k1[j]), _row(attn_lambda_q2[j]),
                       _row(attn_lambda_k2[j]), attn_subln[j].reshape(-1, 1).astype(F32), _lambda_init(i))
            xf = _attn_out(xf, o.reshape(b * s, d), attn_w_out[j].astype(BF16))
        else:
            half = gmlp_w_out.shape[1]
            gw = half // GMLP_GROUPS
            bs_full = jnp.repeat(gmlp_b_s[j].T, gw, axis=1).astype(F32)
            xf = _gmlp(xf, _row(mix_norm[i]), gmlp_w_in[j].astype(BF16), _row(gmlp_b_in[j]), _row(gmlp_ln_g[j]),
                       _row(gmlp_ln_b[j]), gmlp_w_s[j].astype(F32), bs_full, gmlp_w_out[j].astype(BF16),
                       _row(gmlp_b_out[j]))
        xf = _ffn(xf, _row(ffn2_norm[i]), ffn2_w_gate_up[i].astype(BF16), ffn2_w_down[i].astype(BF16))
    return xf.reshape(b, s, d)
```

```python
import functools
import math

import jax
import jax.numpy as jnp
from jax import lax
from jax.experimental import pallas as pl
from jax.experimental.pallas import tpu as pltpu

F32 = jnp.float32
BF16 = jnp.bfloat16

RMS_EPS = 1e-6
LN_EPS = 1e-5
DIFF_HEADS = 8
GMLP_GROUPS = 8
GMLP_CHUNK = 128

MXU_TILE = 256
VMEM_LIMIT_BYTES = 56 * 1024 * 1024
NEG_BIG = -1e30
LOG2_E = math.log2(math.e)
SCORE_BOUND_LOG2 = 60.0

TOKEN_TILE = 512
ATTN_TILE = 512
KV_BLOCKS_PER_STEP = 2


def _const_spec(shape):
    return pl.BlockSpec(shape, lambda *_: (0,) * len(shape), pipeline_mode=pl.Buffered(1))


def _params(n_axes):
    return pltpu.CompilerParams(
        dimension_semantics=("arbitrary",) * n_axes,
        vmem_limit_bytes=VMEM_LIMIT_BYTES,
    )


def _rms_rows(x, gain):
    ms = jnp.mean(x * x, axis=-1, keepdims=True)
    return x * lax.rsqrt(ms + RMS_EPS) * gain


def _ffn_kernel(x_ref, g_ref, wgu_ref, wd_ref, o_ref, h_ref, *, d_ff):
    xn = _rms_rows(x_ref[...], g_ref[...]).astype(BF16)
    for j in range(d_ff // MXU_TILE):
        lo = j * MXU_TILE
        gate = jnp.dot(xn, wgu_ref[:, lo:lo + MXU_TILE], preferred_element_type=F32)
        up = jnp.dot(xn, wgu_ref[:, d_ff + lo:d_ff + lo + MXU_TILE], preferred_element_type=F32)
        h_ref[:, lo:lo + MXU_TILE] = (gate * jax.nn.sigmoid(gate) * up).astype(BF16)
    y = jnp.dot(h_ref[...], wd_ref[...], preferred_element_type=F32)
    o_ref[...] = x_ref[...] + 0.5 * y


def _ffn(x, gain, w_gate_up, w_down):
    t, d = x.shape
    d_ff = w_down.shape[0]
    assert d_ff % MXU_TILE == 0 and t % TOKEN_TILE == 0
    tm = TOKEN_TILE
    return pl.pallas_call(
        functools.partial(_ffn_kernel, d_ff=d_ff),
        out_shape=jax.ShapeDtypeStruct((t, d), F32),
        grid=(t // tm,),
        in_specs=[
            pl.BlockSpec((tm, d), lambda i: (i, 0)),
            _const_spec((1, d)),
            _const_spec((d, 2 * d_ff)),
            _const_spec((d_ff, d)),
        ],
        out_specs=pl.BlockSpec((tm, d), lambda i: (i, 0)),
        scratch_shapes=[pltpu.VMEM((tm, d_ff), BF16)],
        compiler_params=_params(1),
        name="ffn",
    )(x, gain, w_gate_up, w_down)


def _group_ones(n, group):
    r = lax.broadcasted_iota(jnp.int32, (n, n), 0) // group
    c = lax.broadcasted_iota(jnp.int32, (n, n), 1) // group
    return (r == c).astype(BF16)


def _attn_in_kernel(x_ref, g_ref, wk_ref, wqt_ref, wvt_ref, kgain_ref, k_ref, qt_ref, vt_ref, *, hd):
    d = x_ref.shape[1]
    xn = _rms_rows(x_ref[...], g_ref[...]).astype(BF16)
    ones = _group_ones(MXU_TILE, hd)
    nt_dims = (((1,), (1,)), ((), ()))

    k = jnp.dot(xn, wk_ref[...], preferred_element_type=F32)
    for j in range(d // MXU_TILE):
        lo = j * MXU_TILE
        kj = k[:, lo:lo + MXU_TILE]
        ss = jnp.dot((kj * kj).astype(BF16), ones, preferred_element_type=F32)
        kn = kj * lax.rsqrt(ss * (1.0 / hd) + RMS_EPS) * kgain_ref[:, lo:lo + MXU_TILE]
        k_ref[:, lo:lo + MXU_TILE] = kn.astype(BF16)

    qt = lax.dot_general(wqt_ref[...], xn, nt_dims, preferred_element_type=F32)
    for j in range(d // MXU_TILE):
        lo = j * MXU_TILE
        qj = qt[lo:lo + MXU_TILE, :]
        ss = jnp.dot(ones, (qj * qj).astype(BF16), preferred_element_type=F32)
        qt_ref[0, lo:lo + MXU_TILE, :] = (qj * lax.rsqrt(ss * (1.0 / hd) + RMS_EPS)).astype(BF16)

    vt = lax.dot_general(wvt_ref[...], xn, nt_dims, preferred_element_type=F32)
    vt_ref[0] = vt.astype(BF16)


def _attn_in(x, gain, wk, wqt, wvt, kgain, hd):
    t, d = x.shape
    tm = ATTN_TILE
    nt = t // tm
    return pl.pallas_call(
        functools.partial(_attn_in_kernel, hd=hd),
        out_shape=(
            jax.ShapeDtypeStruct((t, d), BF16),
            jax.ShapeDtypeStruct((nt, d, tm), BF16),
            jax.ShapeDtypeStruct((nt, d, tm), BF16),
        ),
        grid=(nt,),
        in_specs=[
            pl.BlockSpec((tm, d), lambda i: (i, 0)),
            _const_spec((1, d)),
            _const_spec((d, d)),
            _const_spec((d, d)),
            _const_spec((d, d)),
            _const_spec((1, d)),
        ],
        out_specs=(
            pl.BlockSpec((tm, d), lambda i: (i, 0)),
            pl.BlockSpec((1, d, tm), lambda i: (i, 0, 0)),
            pl.BlockSpec((1, d, tm), lambda i: (i, 0, 0)),
        ),
        compiler_params=_params(1),
        name="attn_in",
    )(x, gain, wk, wqt, wvt, kgain)


def _flash_kernel(stable_ref, qt_ref, k_ref, vt_ref, lq1_ref, lk1_ref, lq2_ref, lk2_ref, sub_ref, o_ref,
                  q_scr, m_scr, l_scr, acc_scr, *, hd, lam_init):
    i = pl.program_id(2)
    tq = qt_ref.shape[3]
    tk = vt_ref.shape[3]

    qt = qt_ref[0, 0]
    row = lax.broadcasted_iota(jnp.int32, qt.shape, 0)
    zero = jnp.zeros_like(qt)
    q_scr[0] = jnp.where(row < hd, qt, zero)
    q_scr[1] = jnp.where(row >= hd, qt, zero)
    m_scr[...] = jnp.full(m_scr.shape, NEG_BIG, F32)
    l_scr[...] = jnp.zeros(l_scr.shape, F32)
    acc_scr[...] = jnp.zeros(acc_scr.shape, F32)

    def kv_block(j, diagonal, stable, nblk=1):
        start = pl.multiple_of(j * tk, tk)
        kj = k_ref[0, pl.ds(start, nblk * tk), :]
        vj = jnp.concatenate([vt_ref[0, j + n] for n in range(nblk)], axis=1)
        for c in range(2):
            s = jnp.dot(kj, q_scr[c], preferred_element_type=F32)
            if diagonal:
                kpos = lax.broadcasted_iota(jnp.int32, s.shape, 0)
                qpos = lax.broadcasted_iota(jnp.int32, s.shape, 1)
                s = jnp.where(kpos <= qpos, s, NEG_BIG)
            if stable:
                m_old = m_scr[c]
                m_new = jnp.maximum(m_old, jnp.max(s, axis=0, keepdims=True))
                alpha = jnp.exp2(m_old - m_new)
                p = jnp.exp2(s - m_new)
                l_scr[c] = alpha * l_scr[c] + jnp.sum(p, axis=0, keepdims=True)
                acc_scr[c] = alpha * acc_scr[c] + jnp.dot(vj, p.astype(BF16), preferred_element_type=F32)
                m_scr[c] = m_new
            else:
                p = jnp.exp2(s)
                l_scr[c] = l_scr[c] + jnp.sum(p, axis=0, keepdims=True)
                acc_scr[c] = acc_scr[c] + jnp.dot(vj, p.astype(BF16), preferred_element_type=F32)

    def run(stable, nblk):
        def body(jj, carry):
            kv_block(jj * nblk, diagonal=False, stable=stable, nblk=nblk)
            return carry
        n_full = i // nblk
        lax.fori_loop(0, n_full, body, 0)
        for r in range(1, nblk):
            @pl.when(i - n_full * nblk >= r)
            def _():
                kv_block(n_full * nblk + (r - 1), diagonal=False, stable=stable)
        kv_block(i, diagonal=True, stable=stable)

    @pl.when(stable_ref[0] == 0)
    def _():
        run(stable=False, nblk=KV_BLOCKS_PER_STEP)

    @pl.when(stable_ref[0] != 0)
    def _():
        run(stable=True, nblk=1)

    lam = (jnp.exp(jnp.sum(lq1_ref[...] * lk1_ref[...], axis=-1, keepdims=True))
           - jnp.exp(jnp.sum(lq2_ref[...] * lk2_ref[...], axis=-1, keepdims=True))
           + lam_init)
    o = acc_scr[0] * (1.0 / l_scr[0]) - acc_scr[1] * (lam / l_scr[1])
    ms = jnp.mean(o * o, axis=0, keepdims=True)
    on = o * lax.rsqrt(ms + RMS_EPS) * (sub_ref[...] * (1.0 - lam_init))
    o_ref[0] = on.T.astype(BF16)


def _flash(stable, qt, k, vt, lq1, lk1, lq2, lk2, subln_col, lam_init):
    b, nq, d, tq = qt.shape
    _, s, _ = k.shape
    nk, tk = vt.shape[1], vt.shape[3]
    assert tq == tk and nq == nk
    h = DIFF_HEADS
    hw = d // h
    hd = hw // 2
    vec = pl.BlockSpec((1, hd), lambda bb, hh, ii, st: (0, 0))
    grid_spec = pltpu.PrefetchScalarGridSpec(
        num_scalar_prefetch=1,
        grid=(b, h, nq),
        in_specs=[
            pl.BlockSpec((1, 1, hw, tq), lambda bb, hh, ii, st: (bb, ii, hh, 0)),
            pl.BlockSpec((1, s, hw), lambda bb, hh, ii, st: (bb, 0, hh)),
            pl.BlockSpec((1, nk, hw, tk), lambda bb, hh, ii, st: (bb, 0, hh, 0)),
            vec, vec, vec, vec,
            pl.BlockSpec((hw, 1), lambda bb, hh, ii, st: (0, 0)),
        ],
        out_specs=pl.BlockSpec((1, tq, hw), lambda bb, hh, ii, st: (bb, ii, hh)),
        scratch_shapes=[
            pltpu.VMEM((2, hw, tq), BF16),
            pltpu.VMEM((2, 1, tq), F32),
            pltpu.VMEM((2, 1, tq), F32),
            pltpu.VMEM((2, hw, tq), F32),
        ],
    )
    return pl.pallas_call(
        functools.partial(_flash_kernel, hd=hd, lam_init=lam_init),
        out_shape=jax.ShapeDtypeStruct((b, s, d), BF16),
        grid_spec=grid_spec,
        compiler_params=_params(3),
        name="flash",
    )(stable, qt, k, vt, lq1, lk1, lq2, lk2, subln_col)


def _attn_out_kernel(x_ref, o_ref, w_ref, y_ref):
    y_ref[...] = x_ref[...] + jnp.dot(o_ref[...], w_ref[...], preferred_element_type=F32)


def _attn_out(x, o, w_out):
    t, d = x.shape
    tm = TOKEN_TILE
    return pl.pallas_call(
        _attn_out_kernel,
        out_shape=jax.ShapeDtypeStruct((t, d), F32),
        grid=(t // tm,),
        in_specs=[
            pl.BlockSpec((tm, d), lambda i: (i, 0)),
            pl.BlockSpec((tm, d), lambda i: (i, 0)),
            _const_spec((d, d)),
        ],
        out_specs=pl.BlockSpec((tm, d), lambda i: (i, 0)),
        compiler_params=_params(1),
        name="attn_out",
    )(x, o, w_out)


def _gelu(z):
    return 0.5 * z * (1.0 + lax.erf(z * math.sqrt(0.5)))


def _gmlp_kernel(x_ref, g_ref, win_ref, bin_ref, lng_ref, lnb_ref, ws_ref, bs_ref, wout_ref, bout_ref,
                 y_ref, v_scr, gated_scr, *, half):
    tm = x_ref.shape[0]
    gw = half // GMLP_GROUPS
    xn = _rms_rows(x_ref[...], g_ref[...]).astype(BF16)

    rsum = jnp.zeros((tm, 1), F32)
    rsq = jnp.zeros((tm, 1), F32)
    for g in range(GMLP_GROUPS):
        lo = half + g * gw
        z = jnp.dot(xn, win_ref[:, lo:lo + gw], preferred_element_type=F32) + bin_ref[:, lo:lo + gw]
        v = _gelu(z)
        v_scr[:, g * gw:(g + 1) * gw] = v
        rsum = rsum + jnp.sum(v, axis=-1, keepdims=True)
        rsq = rsq + jnp.sum(v * v, axis=-1, keepdims=True)
    mu = rsum * (1.0 / half)
    var = rsq * (1.0 / half) - mu * mu
    rstd = lax.rsqrt(var + LN_EPS)

    t_idx = lax.broadcasted_iota(jnp.int32, (GMLP_CHUNK, GMLP_CHUNK), 0)
    s_idx = lax.broadcasted_iota(jnp.int32, (GMLP_CHUNK, GMLP_CHUNK), 1)
    causal = s_idx <= t_idx

    for g in range(GMLP_GROUPS):
        lo = g * gw
        vn = ((v_scr[:, lo:lo + gw] - mu) * rstd * lng_ref[:, lo:lo + gw] + lnb_ref[:, lo:lo + gw]).astype(BF16)
        w = jnp.where(causal, ws_ref[g], 0.0).astype(BF16)
        z = jnp.dot(xn, win_ref[:, lo:lo + gw], preferred_element_type=F32) + bin_ref[:, lo:lo + gw]
        u = _gelu(z)
        for c in range(tm // GMLP_CHUNK):
            r = c * GMLP_CHUNK
            s = jnp.dot(w, vn[r:r + GMLP_CHUNK, :], preferred_element_type=F32) + bs_ref[:, lo:lo + gw]
            gated_scr[r:r + GMLP_CHUNK, lo:lo + gw] = (u[r:r + GMLP_CHUNK, :] * s).astype(BF16)

    y = jnp.dot(gated_scr[...], wout_ref[...], preferred_element_type=F32)
    y_ref[...] = x_ref[...] + y + bout_ref[...]


def _gmlp(x, gain, w_in, b_in, ln_g, ln_b, w_s, bs_full, w_out, b_out):
    t, d = x.shape
    half = w_out.shape[0]
    tm = TOKEN_TILE
    assert tm % GMLP_CHUNK == 0
    return pl.pallas_call(
        functools.partial(_gmlp_kernel, half=half),
        out_shape=jax.ShapeDtypeStruct((t, d), F32),
        grid=(t // tm,),
        in_specs=[
            pl.BlockSpec((tm, d), lambda i: (i, 0)),
            _const_spec((1, d)),
            _const_spec((d, 2 * half)),
            _const_spec((1, 2 * half)),
            _const_spec((1, half)),
            _const_spec((1, half)),
            _const_spec((GMLP_GROUPS, GMLP_CHUNK, GMLP_CHUNK)),
            _const_spec((GMLP_CHUNK, half)),
            _const_spec((half, d)),
            _const_spec((1, d)),
        ],
        out_specs=pl.BlockSpec((tm, d), lambda i: (i, 0)),
        scratch_shapes=[pltpu.VMEM((tm, half), F32), pltpu.VMEM((tm, half), BF16)],
        compiler_params=_params(1),
        name="gmlp",
    )(x, gain, w_in, b_in, ln_g, ln_b, w_s, bs_full, w_out, b_out)


def _lambda_init(layer_idx):
    return 0.8 - 0.6 * math.exp(-0.3 * layer_idx)


def _row(v):
    return v.reshape(1, -1).astype(F32)


def kernel(x, ffn1_norm, ffn1_w_gate_up, ffn1_w_down, mix_norm, ffn2_norm, ffn2_w_gate_up, ffn2_w_down, attn_w_in, attn_w_out, attn_q_norm, attn_k_norm, attn_lambda_q1, attn_lambda_k1, attn_lambda_q2, attn_lambda_k2, attn_subln, gmlp_w_in, gmlp_b_in, gmlp_ln_g, gmlp_ln_b, gmlp_w_s, gmlp_b_s, gmlp_w_out, gmlp_b_out):
    b, s, d = x.shape
    depth = ffn1_norm.shape[0]
    h = DIFF_HEADS
    hd = d // h // 2
    qk_width = h * 2 * hd
    nq = s // ATTN_TILE
    xf = x.reshape(b * s, d)

    for i in range(depth):
        xf = _ffn(xf, _row(ffn1_norm[i]), ffn1_w_gate_up[i].astype(BF16), ffn1_w_down[i].astype(BF16))
        j = i // 2
        if i % 2 == 0:
            w_in = attn_w_in[j].astype(BF16)
            wqt = w_in[:, :qk_width].T
            wk = w_in[:, qk_width:2 * qk_width]
            wvt = w_in[:, 2 * qk_width:].T
            feat_gain = attn_q_norm[j].astype(F32) * attn_k_norm[j].astype(F32) * (hd ** -0.5 * LOG2_E)
            kgain = jnp.tile(feat_gain, 2 * h).reshape(1, -1)
            score_bound = 1.01 * hd * jnp.max(jnp.abs(feat_gain))
            stable = (score_bound > SCORE_BOUND_LOG2).astype(jnp.int32).reshape(1)
            k, qt, vt = _attn_in(xf, _row(mix_norm[i]), wk, wqt, wvt, kgain, hd)
            o = _flash(stable, qt.reshape(b, nq, d, ATTN_TILE), k.reshape(b, s, d), vt.reshape(b, nq, d, ATTN_TILE),
                       _row(attn_lambda_q1[j]), _row(attn_lambda_k1[j]), _row(attn_lambda_q2[j]),
                       _row(attn_lambda_k2[j]), attn_subln[j].reshape(-1, 1).astype(F32), _lambda_init(i))
            xf = _attn_out(xf, o.reshape(b * s, d), attn_w_out[j].astype(BF16))
        else:
            half = gmlp_w_out.shape[1]
            gw = half // GMLP_GROUPS
            bs_full = jnp.repeat(gmlp_b_s[j].T, gw, axis=1).astype(F32)
            xf = _gmlp(xf, _row(mix_norm[i]), gmlp_w_in[j].astype(BF16), _row(gmlp_b_in[j]), _row(gmlp_ln_g[j]),
                       _row(gmlp_ln_b[j]), gmlp_w_s[j].astype(F32), bs_full, gmlp_w_out[j].astype(BF16),
                       _row(gmlp_b_out[j]))
        xf = _ffn(xf, _row(ffn2_norm[i]), ffn2_w_gate_up[i].astype(BF16), ffn2_w_down[i].astype(BF16))
    return xf.reshape(b, s, d)
```

```python
import functools
import math

import jax
import jax.numpy as jnp
from jax import lax
from jax.experimental import pallas as pl
from jax.experimental.pallas import tpu as pltpu

F32 = jnp.float32
BF16 = jnp.bfloat16

RMS_EPS = 1e-6
LN_EPS = 1e-5
DIFF_HEADS = 8
GMLP_GROUPS = 8
GMLP_CHUNK = 128

MXU_TILE = 256
VMEM_LIMIT_BYTES = 56 * 1024 * 1024
NEG_BIG = -1e30
LOG2_E = math.log2(math.e)
SCORE_BOUND_LOG2 = 60.0

TOKEN_TILE = 512
ATTN_TILE = 512
KV_BLOCKS_PER_STEP = 4


def _const_spec(shape):
    return pl.BlockSpec(shape, lambda *_: (0,) * len(shape), pipeline_mode=pl.Buffered(1))


def _params(n_axes):
    return pltpu.CompilerParams(
        dimension_semantics=("arbitrary",) * n_axes,
        vmem_limit_bytes=VMEM_LIMIT_BYTES,
    )


def _rms_rows(x, gain):
    ms = jnp.mean(x * x, axis=-1, keepdims=True)
    return x * lax.rsqrt(ms + RMS_EPS) * gain


def _ffn_kernel(x_ref, g_ref, wgu_ref, wd_ref, o_ref, h_ref, *, d_ff):
    xn = _rms_rows(x_ref[...], g_ref[...]).astype(BF16)
    for j in range(d_ff // MXU_TILE):
        lo = j * MXU_TILE
        gate = jnp.dot(xn, wgu_ref[:, lo:lo + MXU_TILE], preferred_element_type=F32)
        up = jnp.dot(xn, wgu_ref[:, d_ff + lo:d_ff + lo + MXU_TILE], preferred_element_type=F32)
        h_ref[:, lo:lo + MXU_TILE] = (gate * jax.nn.sigmoid(gate) * up).astype(BF16)
    y = jnp.dot(h_ref[...], wd_ref[...], preferred_element_type=F32)
    o_ref[...] = x_ref[...] + 0.5 * y


def _ffn(x, gain, w_gate_up, w_down):
    t, d = x.shape
    d_ff = w_down.shape[0]
    assert d_ff % MXU_TILE == 0 and t % TOKEN_TILE == 0
    tm = TOKEN_TILE
    return pl.pallas_call(
        functools.partial(_ffn_kernel, d_ff=d_ff),
        out_shape=jax.ShapeDtypeStruct((t, d), F32),
        grid=(t // tm,),
        in_specs=[
            pl.BlockSpec((tm, d), lambda i: (i, 0)),
            _const_spec((1, d)),
            _const_spec((d, 2 * d_ff)),
            _const_spec((d_ff, d)),
        ],
        out_specs=pl.BlockSpec((tm, d), lambda i: (i, 0)),
        scratch_shapes=[pltpu.VMEM((tm, d_ff), BF16)],
        compiler_params=_params(1),
        name="ffn",
    )(x, gain, w_gate_up, w_down)


def _group_ones(n, group):
    r = lax.broadcasted_iota(jnp.int32, (n, n), 0) // group
    c = lax.broadcasted_iota(jnp.int32, (n, n), 1) // group
    return (r == c).astype(BF16)


def _attn_in_kernel(x_ref, g_ref, wk_ref, wqt_ref, wvt_ref, kgain_ref, k_ref, qt_ref, vt_ref, *, hd):
    d = x_ref.shape[1]
    xn = _rms_rows(x_ref[...], g_ref[...]).astype(BF16)
    ones = _group_ones(MXU_TILE, hd)
    nt_dims = (((1,), (1,)), ((), ()))

    k = jnp.dot(xn, wk_ref[...], preferred_element_type=F32)
    for j in range(d // MXU_TILE):
        lo = j * MXU_TILE
        kj = k[:, lo:lo + MXU_TILE]
        ss = jnp.dot((kj * kj).astype(BF16), ones, preferred_element_type=F32)
        kn = kj * lax.rsqrt(ss * (1.0 / hd) + RMS_EPS) * kgain_ref[:, lo:lo + MXU_TILE]
        k_ref[:, lo:lo + MXU_TILE] = kn.astype(BF16)

    qt = lax.dot_general(wqt_ref[...], xn, nt_dims, preferred_element_type=F32)
    for j in range(d // MXU_TILE):
        lo = j * MXU_TILE
        qj = qt[lo:lo + MXU_TILE, :]
        ss = jnp.dot(ones, (qj * qj).astype(BF16), preferred_element_type=F32)
        qt_ref[0, lo:lo + MXU_TILE, :] = (qj * lax.rsqrt(ss * (1.0 / hd) + RMS_EPS)).astype(BF16)

    vt = lax.dot_general(wvt_ref[...], xn, nt_dims, preferred_element_type=F32)
    vt_ref[0] = vt.astype(BF16)


def _attn_in(x, gain, wk, wqt, wvt, kgain, hd):
    t, d = x.shape
    tm = ATTN_TILE
    nt = t // tm
    return pl.pallas_call(
        functools.partial(_attn_in_kernel, hd=hd),
        out_shape=(
            jax.ShapeDtypeStruct((t, d), BF16),
            jax.ShapeDtypeStruct((nt, d, tm), BF16),
            jax.ShapeDtypeStruct((nt, d, tm), BF16),
        ),
        grid=(nt,),
        in_specs=[
            pl.BlockSpec((tm, d), lambda i: (i, 0)),
            _const_spec((1, d)),
            _const_spec((d, d)),
            _const_spec((d, d)),
            _const_spec((d, d)),
            _const_spec((1, d)),
        ],
        out_specs=(
            pl.BlockSpec((tm, d), lambda i: (i, 0)),
            pl.BlockSpec((1, d, tm), lambda i: (i, 0, 0)),
            pl.BlockSpec((1, d, tm), lambda i: (i, 0, 0)),
        ),
        compiler_params=_params(1),
        name="attn_in",
    )(x, gain, wk, wqt, wvt, kgain)


def _flash_kernel(stable_ref, qt_ref, k_ref, vt_ref, lq1_ref, lk1_ref, lq2_ref, lk2_ref, sub_ref, o_ref,
                  q_scr, m_scr, l_scr, acc_scr, *, hd, lam_init):
    i = pl.program_id(2)
    tq = qt_ref.shape[3]
    tk = vt_ref.shape[3]

    qt = qt_ref[0, 0]
    row = lax.broadcasted_iota(jnp.int32, qt.shape, 0)
    zero = jnp.zeros_like(qt)
    q_scr[0] = jnp.where(row < hd, qt, zero)
    q_scr[1] = jnp.where(row >= hd, qt, zero)
    m_scr[...] = jnp.full(m_scr.shape, NEG_BIG, F32)
    l_scr[...] = jnp.zeros(l_scr.shape, F32)
    acc_scr[...] = jnp.zeros(acc_scr.shape, F32)

    def kv_block(j, diagonal, stable, nblk=1):
        start = pl.multiple_of(j * tk, tk)
        kj = k_ref[0, pl.ds(start, nblk * tk), :]
        vj = jnp.concatenate([vt_ref[0, j + n] for n in range(nblk)], axis=1)
        for c in range(2):
            s = jnp.dot(kj, q_scr[c], preferred_element_type=F32)
            if diagonal:
                kpos = lax.broadcasted_iota(jnp.int32, s.shape, 0) - (nblk - 1) * tk
                qpos = lax.broadcasted_iota(jnp.int32, s.shape, 1)
                s = jnp.where(kpos <= qpos, s, NEG_BIG)
            if stable:
                m_old = m_scr[c]
                m_new = jnp.maximum(m_old, jnp.max(s, axis=0, keepdims=True))
                alpha = jnp.exp2(m_old - m_new)
                p = jnp.exp2(s - m_new)
                l_scr[c] = alpha * l_scr[c] + jnp.sum(p, axis=0, keepdims=True)
                acc_scr[c] = alpha * acc_scr[c] + jnp.dot(vj, p.astype(BF16), preferred_element_type=F32)
                m_scr[c] = m_new
            else:
                p = jnp.exp2(s)
                l_scr[c] = l_scr[c] + jnp.sum(p, axis=0, keepdims=True)
                acc_scr[c] = acc_scr[c] + jnp.dot(vj, p.astype(BF16), preferred_element_type=F32)

    def run(stable, nblk):
        def body(jj, carry):
            kv_block(jj * nblk, diagonal=False, stable=stable, nblk=nblk)
            return carry
        n_full = i // nblk
        lax.fori_loop(0, n_full, body, 0)
        for r in range(nblk):
            @pl.when(i - n_full * nblk == r)
            def _():
                kv_block(n_full * nblk, diagonal=True, stable=stable, nblk=r + 1)

    @pl.when(stable_ref[0] == 0)
    def _():
        run(stable=False, nblk=KV_BLOCKS_PER_STEP)

    @pl.when(stable_ref[0] != 0)
    def _():
        run(stable=True, nblk=1)

    lam = (jnp.exp(jnp.sum(lq1_ref[...] * lk1_ref[...], axis=-1, keepdims=True))
           - jnp.exp(jnp.sum(lq2_ref[...] * lk2_ref[...], axis=-1, keepdims=True))
           + lam_init)
    o = acc_scr[0] * (1.0 / l_scr[0]) - acc_scr[1] * (lam / l_scr[1])
    ms = jnp.mean(o * o, axis=0, keepdims=True)
    on = o * lax.rsqrt(ms + RMS_EPS) * (sub_ref[...] * (1.0 - lam_init))
    o_ref[0] = on.T.astype(BF16)


def _flash(stable, qt, k, vt, lq1, lk1, lq2, lk2, subln_col, lam_init):
    b, nq, d, tq = qt.shape
    _, s, _ = k.shape
    nk, tk = vt.shape[1], vt.shape[3]
    assert tq == tk and nq == nk
    h = DIFF_HEADS
    hw = d // h
    hd = hw // 2
    vec = pl.BlockSpec((1, hd), lambda bb, hh, ii, st: (0, 0))
    grid_spec = pltpu.PrefetchScalarGridSpec(
        num_scalar_prefetch=1,
        grid=(b, h, nq),
        in_specs=[
            pl.BlockSpec((1, 1, hw, tq), lambda bb, hh, ii, st: (bb, ii, hh, 0)),
            pl.BlockSpec((1, s, hw), lambda bb, hh, ii, st: (bb, 0, hh)),
            pl.BlockSpec((1, nk, hw, tk), lambda bb, hh, ii, st: (bb, 0, hh, 0)),
            vec, vec, vec, vec,
            pl.BlockSpec((hw, 1), lambda bb, hh, ii, st: (0, 0)),
        ],
        out_specs=pl.BlockSpec((1, tq, hw), lambda bb, hh, ii, st: (bb, ii, hh)),
        scratch_shapes=[
            pltpu.VMEM((2, hw, tq), BF16),
            pltpu.VMEM((2, 1, tq), F32),
            pltpu.VMEM((2, 1, tq), F32),
            pltpu.VMEM((2, hw, tq), F32),
        ],
    )
    return pl.pallas_call(
        functools.partial(_flash_kernel, hd=hd, lam_init=lam_init),
        out_shape=jax.ShapeDtypeStruct((b, s, d), BF16),
        grid_spec=grid_spec,
        compiler_params=_params(3),
        name="flash",
    )(stable, qt, k, vt, lq1, lk1, lq2, lk2, subln_col)


def _attn_out_kernel(x_ref, o_ref, w_ref, y_ref):
    y_ref[...] = x_ref[...] + jnp.dot(o_ref[...], w_ref[...], preferred_element_type=F32)


def _attn_out(x, o, w_out):
    t, d = x.shape
    tm = TOKEN_TILE
    return pl.pallas_call(
        _attn_out_kernel,
        out_shape=jax.ShapeDtypeStruct((t, d), F32),
        grid=(t // tm,),
        in_specs=[
            pl.BlockSpec((tm, d), lambda i: (i, 0)),
            pl.BlockSpec((tm, d), lambda i: (i, 0)),
            _const_spec((d, d)),
        ],
        out_specs=pl.BlockSpec((tm, d), lambda i: (i, 0)),
        compiler_params=_params(1),
        name="attn_out",
    )(x, o, w_out)


def _gelu(z):
    return 0.5 * z * (1.0 + lax.erf(z * math.sqrt(0.5)))


def _gmlp_kernel(x_ref, g_ref, win_ref, bin_ref, lng_ref, lnb_ref, ws_ref, bs_ref, wout_ref, bout_ref,
                 y_ref, v_scr, gated_scr, *, half):
    tm = x_ref.shape[0]
    gw = half // GMLP_GROUPS
    xn = _rms_rows(x_ref[...], g_ref[...]).astype(BF16)

    rsum = jnp.zeros((tm, 1), F32)
    rsq = jnp.zeros((tm, 1), F32)
    for g in range(GMLP_GROUPS):
        lo = half + g * gw
        z = jnp.dot(xn, win_ref[:, lo:lo + gw], preferred_element_type=F32) + bin_ref[:, lo:lo + gw]
        v = _gelu(z)
        v_scr[:, g * gw:(g + 1) * gw] = v
        rsum = rsum + jnp.sum(v, axis=-1, keepdims=True)
        rsq = rsq + jnp.sum(v * v, axis=-1, keepdims=True)
    mu = rsum * (1.0 / half)
    var = rsq * (1.0 / half) - mu * mu
    rstd = lax.rsqrt(var + LN_EPS)

    t_idx = lax.broadcasted_iota(jnp.int32, (GMLP_CHUNK, GMLP_CHUNK), 0)
    s_idx = lax.broadcasted_iota(jnp.int32, (GMLP_CHUNK, GMLP_CHUNK), 1)
    causal = s_idx <= t_idx

    for g in range(GMLP_GROUPS):
        lo = g * gw
        vn = ((v_scr[:, lo:lo + gw] - mu) * rstd * lng_ref[:, lo:lo + gw] + lnb_ref[:, lo:lo + gw]).astype(BF16)
        w = jnp.where(causal, ws_ref[g], 0.0).astype(BF16)
        z = jnp.dot(xn, win_ref[:, lo:lo + gw], preferred_element_type=F32) + bin_ref[:, lo:lo + gw]
        u = _gelu(z)
        for c in range(tm // GMLP_CHUNK):
            r = c * GMLP_CHUNK
            s = jnp.dot(w, vn[r:r + GMLP_CHUNK, :], preferred_element_type=F32) + bs_ref[:, lo:lo + gw]
            gated_scr[r:r + GMLP_CHUNK, lo:lo + gw] = (u[r:r + GMLP_CHUNK, :] * s).astype(BF16)

    y = jnp.dot(gated_scr[...], wout_ref[...], preferred_element_type=F32)
    y_ref[...] = x_ref[...] + y + bout_ref[...]


def _gmlp(x, gain, w_in, b_in, ln_g, ln_b, w_s, bs_full, w_out, b_out):
    t, d = x.shape
    half = w_out.shape[0]
    tm = TOKEN_TILE
    assert tm % GMLP_CHUNK == 0
    return pl.pallas_call(
        functools.partial(_gmlp_kernel, half=half),
        out_shape=jax.ShapeDtypeStruct((t, d), F32),
        grid=(t // tm,),
        in_specs=[
            pl.BlockSpec((tm, d), lambda i: (i, 0)),
            _const_spec((1, d)),
            _const_spec((d, 2 * half)),
            _const_spec((1, 2 * half)),
            _const_spec((1, half)),
            _const_spec((1, half)),
            _const_spec((GMLP_GROUPS, GMLP_CHUNK, GMLP_CHUNK)),
            _const_spec((GMLP_CHUNK, half)),
            _const_spec((half, d)),
            _const_spec((1, d)),
        ],
        out_specs=pl.BlockSpec((tm, d), lambda i: (i, 0)),
        scratch_shapes=[pltpu.VMEM((tm, half), F32), pltpu.VMEM((tm, half), BF16)],
        compiler_params=_params(1),
        name="gmlp",
    )(x, gain, w_in, b_in, ln_g, ln_b, w_s, bs_full, w_out, b_out)


def _lambda_init(layer_idx):
    return 0.8 - 0.6 * math.exp(-0.3 * layer_idx)


def _row(v):
    return v.reshape(1, -1).astype(F32)


def kernel(x, ffn1_norm, ffn1_w_gate_up, ffn1_w_down, mix_norm, ffn2_norm, ffn2_w_gate_up, ffn2_w_down, attn_w_in, attn_w_out, attn_q_norm, attn_k_norm, attn_lambda_q1, attn_lambda_k1, attn_lambda_q2, attn_lambda_k2, attn_subln, gmlp_w_in, gmlp_b_in, gmlp_ln_g, gmlp_ln_b, gmlp_w_s, gmlp_b_s, gmlp_w_out, gmlp_b_out):
    b, s, d = x.shape
    depth = ffn1_norm.shape[0]
    h = DIFF_HEADS
    hd = d // h // 2
    qk_width = h * 2 * hd
    nq = s // ATTN_TILE
    xf = x.reshape(b * s, d)

    for i in range(depth):
        xf = _ffn(xf, _row(ffn1_norm[i]), ffn1_w_gate_up[i].astype(BF16), ffn1_w_down[i].astype(BF16))
        j = i // 2
        if i % 2 == 0:
            w_in = attn_w_in[j].astype(BF16)
            wqt = w_in[:, :qk_width].T
            wk = w_in[:, qk_width:2 * qk_width]
            wvt = w_in[:, 2 * qk_width:].T
            feat_gain = attn_q_norm[j].astype(F32) * attn_k_norm[j].astype(F32) * (hd ** -0.5 * LOG2_E)
            kgain = jnp.tile(feat_gain, 2 * h).reshape(1, -1)
            score_bound = 1.01 * hd * jnp.max(jnp.abs(feat_gain))
            stable = (score_bound > SCORE_BOUND_LOG2).astype(jnp.int32).reshape(1)
            k, qt, vt = _attn_in(xf, _row(mix_norm[i]), wk, wqt, wvt, kgain, hd)
            o = _flash(stable, qt.reshape(b, nq, d, ATTN_TILE), k.reshape(b, s, d), vt.reshape(b, nq, d, ATTN_TILE),
                       _row(attn_lambda_q1[j]), _row(attn_lambda_k1[j]), _row(attn_lambda_q2[j]),
                       _row(attn_lambda_k2[j]), attn_subln[j].reshape(-1, 1).astype(F32), _lambda_init(i))
            xf = _attn_out(xf, o.reshape(b * s, d), attn_w_out[j].astype(BF16))
        else:
            half = gmlp_w_out.shape[1]
            gw = half // GMLP_GROUPS
            bs_full = jnp.repeat(gmlp_b_s[j].T, gw, axis=1).astype(F32)
            xf = _gmlp(xf, _row(mix_norm[i]), gmlp_w_in[j].astype(BF16), _row(gmlp_b_in[j]), _row(gmlp_ln_g[j]),
                       _row(gmlp_ln_b[j]), gmlp_w_s[j].astype(F32), bs_full, gmlp_w_out[j].astype(BF16),
                       _row(gmlp_b_out[j]))
        xf = _ffn(xf, _row(ffn2_norm[i]), ffn2_w_gate_up[i].astype(BF16), ffn2_w_down[i].astype(BF16))
    return xf.reshape(b, s, d)
```

```python
import functools
import math

import jax
import jax.numpy as jnp
from jax import lax
from jax.experimental import pallas as pl
from jax.experimental.pallas import tpu as pltpu

F32 = jnp.float32
BF16 = jnp.bfloat16

RMS_EPS = 1e-6
LN_EPS = 1e-5
DIFF_HEADS = 8
GMLP_GROUPS = 8
GMLP_CHUNK = 128

MXU_TILE = 256
VMEM_LIMIT_BYTES = 56 * 1024 * 1024
NEG_BIG = -1e30
LOG2_E = math.log2(math.e)
SCORE_BOUND_LOG2 = 60.0

TOKEN_TILE = 512
ATTN_TILE = 512
Q_TILES = 2
KV_BLOCKS_PER_STEP = 4


def _const_spec(shape):
    return pl.BlockSpec(shape, lambda *_: (0,) * len(shape), pipeline_mode=pl.Buffered(1))


def _params(n_axes):
    return pltpu.CompilerParams(
        dimension_semantics=("arbitrary",) * n_axes,
        vmem_limit_bytes=VMEM_LIMIT_BYTES,
    )


def _rms_rows(x, gain):
    ms = jnp.mean(x * x, axis=-1, keepdims=True)
    return x * lax.rsqrt(ms + RMS_EPS) * gain


def _ffn_kernel(x_ref, g_ref, wgu_ref, wd_ref, o_ref, h_ref, *, d_ff):
    xn = _rms_rows(x_ref[...], g_ref[...]).astype(BF16)
    for j in range(d_ff // MXU_TILE):
        lo = j * MXU_TILE
        gate = jnp.dot(xn, wgu_ref[:, lo:lo + MXU_TILE], preferred_element_type=F32)
        up = jnp.dot(xn, wgu_ref[:, d_ff + lo:d_ff + lo + MXU_TILE], preferred_element_type=F32)
        h_ref[:, lo:lo + MXU_TILE] = (gate * jax.nn.sigmoid(gate) * up).astype(BF16)
    y = jnp.dot(h_ref[...], wd_ref[...], preferred_element_type=F32)
    o_ref[...] = x_ref[...] + 0.5 * y


def _ffn(x, gain, w_gate_up, w_down):
    t, d = x.shape
    d_ff = w_down.shape[0]
    assert d_ff % MXU_TILE == 0 and t % TOKEN_TILE == 0
    tm = TOKEN_TILE
    return pl.pallas_call(
        functools.partial(_ffn_kernel, d_ff=d_ff),
        out_shape=jax.ShapeDtypeStruct((t, d), F32),
        grid=(t // tm,),
        in_specs=[
            pl.BlockSpec((tm, d), lambda i: (i, 0)),
            _const_spec((1, d)),
            _const_spec((d, 2 * d_ff)),
            _const_spec((d_ff, d)),
        ],
        out_specs=pl.BlockSpec((tm, d), lambda i: (i, 0)),
        scratch_shapes=[pltpu.VMEM((tm, d_ff), BF16)],
        compiler_params=_params(1),
        name="ffn",
    )(x, gain, w_gate_up, w_down)


def _group_ones(n, group):
    r = lax.broadcasted_iota(jnp.int32, (n, n), 0) // group
    c = lax.broadcasted_iota(jnp.int32, (n, n), 1) // group
    return (r == c).astype(BF16)


def _attn_in_kernel(x_ref, g_ref, wk_ref, wqt_ref, wvt_ref, kgain_ref, k_ref, qt_ref, vt_ref, *, hd):
    d = x_ref.shape[1]
    xn = _rms_rows(x_ref[...], g_ref[...]).astype(BF16)
    ones = _group_ones(MXU_TILE, hd)
    nt_dims = (((1,), (1,)), ((), ()))

    k = jnp.dot(xn, wk_ref[...], preferred_element_type=F32)
    for j in range(d // MXU_TILE):
        lo = j * MXU_TILE
        kj = k[:, lo:lo + MXU_TILE]
        ss = jnp.dot((kj * kj).astype(BF16), ones, preferred_element_type=F32)
        kn = kj * lax.rsqrt(ss * (1.0 / hd) + RMS_EPS) * kgain_ref[:, lo:lo + MXU_TILE]
        k_ref[:, lo:lo + MXU_TILE] = kn.astype(BF16)

    qt = lax.dot_general(wqt_ref[...], xn, nt_dims, preferred_element_type=F32)
    for j in range(d // MXU_TILE):
        lo = j * MXU_TILE
        qj = qt[lo:lo + MXU_TILE, :]
        ss = jnp.dot(ones, (qj * qj).astype(BF16), preferred_element_type=F32)
        qt_ref[0, lo:lo + MXU_TILE, :] = (qj * lax.rsqrt(ss * (1.0 / hd) + RMS_EPS)).astype(BF16)

    vt = lax.dot_general(wvt_ref[...], xn, nt_dims, preferred_element_type=F32)
    vt_ref[0] = vt.astype(BF16)


def _attn_in(x, gain, wk, wqt, wvt, kgain, hd):
    t, d = x.shape
    tm = ATTN_TILE
    nt = t // tm
    return pl.pallas_call(
        functools.partial(_attn_in_kernel, hd=hd),
        out_shape=(
            jax.ShapeDtypeStruct((t, d), BF16),
            jax.ShapeDtypeStruct((nt, d, tm), BF16),
            jax.ShapeDtypeStruct((nt, d, tm), BF16),
        ),
        grid=(nt,),
        in_specs=[
            pl.BlockSpec((tm, d), lambda i: (i, 0)),
            _const_spec((1, d)),
            _const_spec((d, d)),
            _const_spec((d, d)),
            _const_spec((d, d)),
            _const_spec((1, d)),
        ],
        out_specs=(
            pl.BlockSpec((tm, d), lambda i: (i, 0)),
            pl.BlockSpec((1, d, tm), lambda i: (i, 0, 0)),
            pl.BlockSpec((1, d, tm), lambda i: (i, 0, 0)),
        ),
        compiler_params=_params(1),
        name="attn_in",
    )(x, gain, wk, wqt, wvt, kgain)


def _flash_kernel(stable_ref, qt_ref, k_ref, vt_ref, lq1_ref, lk1_ref, lq2_ref, lk2_ref, sub_ref, o_ref,
                  q_scr, m_scr, l_scr, acc_scr, *, hd, lam_init):
    i = pl.program_id(2)
    tk = vt_ref.shape[3]
    tq = Q_TILES * tk

    qt = jnp.concatenate([qt_ref[0, n] for n in range(Q_TILES)], axis=1)
    row = lax.broadcasted_iota(jnp.int32, qt.shape, 0)
    zero = jnp.zeros_like(qt)
    q_scr[0] = jnp.where(row < hd, qt, zero)
    q_scr[1] = jnp.where(row >= hd, qt, zero)
    m_scr[...] = jnp.full(m_scr.shape, NEG_BIG, F32)
    l_scr[...] = jnp.zeros(l_scr.shape, F32)
    acc_scr[...] = jnp.zeros(acc_scr.shape, F32)

    def kv_block(j, nblk, q_lo, diagonal, stable):
        start = pl.multiple_of(j * tk, tk)
        kj = k_ref[0, pl.ds(start, nblk * tk), :]
        vj = jnp.concatenate([vt_ref[0, j + n] for n in range(nblk)], axis=1)
        for c in range(2):
            s = jnp.dot(kj, q_scr[c, :, q_lo:], preferred_element_type=F32)
            if diagonal:
                kpos = lax.broadcasted_iota(jnp.int32, s.shape, 0) - (nblk - 1) * tk
                qpos = lax.broadcasted_iota(jnp.int32, s.shape, 1)
                s = jnp.where(kpos <= qpos, s, NEG_BIG)
            if stable:
                m_old = m_scr[c, :, q_lo:]
                m_new = jnp.maximum(m_old, jnp.max(s, axis=0, keepdims=True))
                alpha = jnp.exp2(m_old - m_new)
                p = jnp.exp2(s - m_new)
                l_scr[c, :, q_lo:] = alpha * l_scr[c, :, q_lo:] + jnp.sum(p, axis=0, keepdims=True)
                acc_scr[c, :, q_lo:] = (alpha * acc_scr[c, :, q_lo:]
                                        + jnp.dot(vj, p.astype(BF16), preferred_element_type=F32))
                m_scr[c, :, q_lo:] = m_new
            else:
                p = jnp.exp2(s)
                l_scr[c, :, q_lo:] = l_scr[c, :, q_lo:] + jnp.sum(p, axis=0, keepdims=True)
                acc_scr[c, :, q_lo:] = acc_scr[c, :, q_lo:] + jnp.dot(vj, p.astype(BF16), preferred_element_type=F32)

    def run(stable, chunk):
        n_below = Q_TILES * i
        n_trips = n_below // chunk

        def body(jj, carry):
            kv_block(jj * chunk, chunk, 0, diagonal=False, stable=stable)
            return carry
        lax.fori_loop(0, n_trips, body, 0)

        for rem in range(0, chunk, Q_TILES):
            @pl.when(n_below - n_trips * chunk == rem)
            def _():
                kv_block(n_trips * chunk, rem + 1, 0, diagonal=True, stable=stable)
                for n in range(1, Q_TILES):
                    kv_block(n_below + n, 1, n * tk, diagonal=True, stable=stable)

    @pl.when(stable_ref[0] == 0)
    def _():
        run(stable=False, chunk=KV_BLOCKS_PER_STEP)

    @pl.when(stable_ref[0] != 0)
    def _():
        run(stable=True, chunk=1)

    lam = (jnp.exp(jnp.sum(lq1_ref[...] * lk1_ref[...], axis=-1, keepdims=True))
           - jnp.exp(jnp.sum(lq2_ref[...] * lk2_ref[...], axis=-1, keepdims=True))
           + lam_init)
    o = acc_scr[0] * (1.0 / l_scr[0]) - acc_scr[1] * (lam / l_scr[1])
    ms = jnp.mean(o * o, axis=0, keepdims=True)
    on = o * lax.rsqrt(ms + RMS_EPS) * (sub_ref[...] * (1.0 - lam_init))
    o_ref[0] = on.T.astype(BF16)


def _flash(stable, qt, k, vt, lq1, lk1, lq2, lk2, subln_col, lam_init):
    b, nk, d, tk = vt.shape
    _, s, _ = k.shape
    assert qt.shape == vt.shape and nk % Q_TILES == 0 and KV_BLOCKS_PER_STEP % Q_TILES == 0
    nq = nk // Q_TILES
    tq = Q_TILES * tk
    h = DIFF_HEADS
    hw = d // h
    hd = hw // 2
    vec = pl.BlockSpec((1, hd), lambda bb, hh, ii, st: (0, 0))
    grid_spec = pltpu.PrefetchScalarGridSpec(
        num_scalar_prefetch=1,
        grid=(b, h, nq),
        in_specs=[
            pl.BlockSpec((1, Q_TILES, hw, tk), lambda bb, hh, ii, st: (bb, ii, hh, 0)),
            pl.BlockSpec((1, s, hw), lambda bb, hh, ii, st: (bb, 0, hh)),
            pl.BlockSpec((1, nk, hw, tk), lambda bb, hh, ii, st: (bb, 0, hh, 0)),
            vec, vec, vec, vec,
            pl.BlockSpec((hw, 1), lambda bb, hh, ii, st: (0, 0)),
        ],
        out_specs=pl.BlockSpec((1, tq, hw), lambda bb, hh, ii, st: (bb, ii, hh)),
        scratch_shapes=[
            pltpu.VMEM((2, hw, tq), BF16),
            pltpu.VMEM((2, 1, tq), F32),
            pltpu.VMEM((2, 1, tq), F32),
            pltpu.VMEM((2, hw, tq), F32),
        ],
    )
    return pl.pallas_call(
        functools.partial(_flash_kernel, hd=hd, lam_init=lam_init),
        out_shape=jax.ShapeDtypeStruct((b, s, d), BF16),
        grid_spec=grid_spec,
        compiler_params=_params(3),
        name="flash",
    )(stable, qt, k, vt, lq1, lk1, lq2, lk2, subln_col)


def _attn_out_kernel(x_ref, o_ref, w_ref, y_ref):
    y_ref[...] = x_ref[...] + jnp.dot(o_ref[...], w_ref[...], preferred_element_type=F32)


def _attn_out(x, o, w_out):
    t, d = x.shape
    tm = TOKEN_TILE
    return pl.pallas_call(
        _attn_out_kernel,
        out_shape=jax.ShapeDtypeStruct((t, d), F32),
        grid=(t // tm,),
        in_specs=[
            pl.BlockSpec((tm, d), lambda i: (i, 0)),
            pl.BlockSpec((tm, d), lambda i: (i, 0)),
            _const_spec((d, d)),
        ],
        out_specs=pl.BlockSpec((tm, d), lambda i: (i, 0)),
        compiler_params=_params(1),
        name="attn_out",
    )(x, o, w_out)


def _gelu(z):
    return 0.5 * z * (1.0 + lax.erf(z * math.sqrt(0.5)))


def _gmlp_kernel(x_ref, g_ref, win_ref, bin_ref, lng_ref, lnb_ref, ws_ref, bs_ref, wout_ref, bout_ref,
                 y_ref, v_scr, gated_scr, *, half):
    tm = x_ref.shape[0]
    gw = half // GMLP_GROUPS
    xn = _rms_rows(x_ref[...], g_ref[...]).astype(BF16)

    rsum = jnp.zeros((tm, 1), F32)
    rsq = jnp.zeros((tm, 1), F32)
    for g in range(GMLP_GROUPS):
        lo = half + g * gw
        z = jnp.dot(xn, win_ref[:, lo:lo + gw], preferred_element_type=F32) + bin_ref[:, lo:lo + gw]
        v = _gelu(z)
        v_scr[:, g * gw:(g + 1) * gw] = v
        rsum = rsum + jnp.sum(v, axis=-1, keepdims=True)
        rsq = rsq + jnp.sum(v * v, axis=-1, keepdims=True)
    mu = rsum * (1.0 / half)
    var = rsq * (1.0 / half) - mu * mu
    rstd = lax.rsqrt(var + LN_EPS)

    t_idx = lax.broadcasted_iota(jnp.int32, (GMLP_CHUNK, GMLP_CHUNK), 0)
    s_idx = lax.broadcasted_iota(jnp.int32, (GMLP_CHUNK, GMLP_CHUNK), 1)
    causal = s_idx <= t_idx

    for g in range(GMLP_GROUPS):
        lo = g * gw
        vn = ((v_scr[:, lo:lo + gw] - mu) * rstd * lng_ref[:, lo:lo + gw] + lnb_ref[:, lo:lo + gw]).astype(BF16)
        w = jnp.where(causal, ws_ref[g], 0.0).astype(BF16)
        z = jnp.dot(xn, win_ref[:, lo:lo + gw], preferred_element_type=F32) + bin_ref[:, lo:lo + gw]
        u = _gelu(z)
        for c in range(tm // GMLP_CHUNK):
            r = c * GMLP_CHUNK
            s = jnp.dot(w, vn[r:r + GMLP_CHUNK, :], preferred_element_type=F32) + bs_ref[:, lo:lo + gw]
            gated_scr[r:r + GMLP_CHUNK, lo:lo + gw] = (u[r:r + GMLP_CHUNK, :] * s).astype(BF16)

    y = jnp.dot(gated_scr[...], wout_ref[...], preferred_element_type=F32)
    y_ref[...] = x_ref[...] + y + bout_ref[...]


def _gmlp(x, gain, w_in, b_in, ln_g, ln_b, w_s, bs_full, w_out, b_out):
    t, d = x.shape
    half = w_out.shape[0]
    tm = TOKEN_TILE
    assert tm % GMLP_CHUNK == 0
    return pl.pallas_call(
        functools.partial(_gmlp_kernel, half=half),
        out_shape=jax.ShapeDtypeStruct((t, d), F32),
        grid=(t // tm,),
        in_specs=[
            pl.BlockSpec((tm, d), lambda i: (i, 0)),
            _const_spec((1, d)),
            _const_spec((d, 2 * half)),
            _const_spec((1, 2 * half)),
            _const_spec((1, half)),
            _const_spec((1, half)),
            _const_spec((GMLP_GROUPS, GMLP_CHUNK, GMLP_CHUNK)),
            _const_spec((GMLP_CHUNK, half)),
            _const_spec((half, d)),
            _const_spec((1, d)),
        ],
        out_specs=pl.BlockSpec((tm, d), lambda i: (i, 0)),
        scratch_shapes=[pltpu.VMEM((tm, half), F32), pltpu.VMEM((tm, half), BF16)],
        compiler_params=_params(1),
        name="gmlp",
    )(x, gain, w_in, b_in, ln_g, ln_b, w_s, bs_full, w_out, b_out)


def _lambda_init(layer_idx):
    return 0.8 - 0.6 * math.exp(-0.3 * layer_idx)


def _row(v):
    return v.reshape(1, -1).astype(F32)


def kernel(x, ffn1_norm, ffn1_w_gate_up, ffn1_w_down, mix_norm, ffn2_norm, ffn2_w_gate_up, ffn2_w_down, attn_w_in, attn_w_out, attn_q_norm, attn_k_norm, attn_lambda_q1, attn_lambda_k1, attn_lambda_q2, attn_lambda_k2, attn_subln, gmlp_w_in, gmlp_b_in, gmlp_ln_g, gmlp_ln_b, gmlp_w_s, gmlp_b_s, gmlp_w_out, gmlp_b_out):
    b, s, d = x.shape
    depth = ffn1_norm.shape[0]
    h = DIFF_HEADS
    hd = d // h // 2
    qk_width = h * 2 * hd
    nq = s // ATTN_TILE
    xf = x.reshape(b * s, d)

    for i in range(depth):
        xf = _ffn(xf, _row(ffn1_norm[i]), ffn1_w_gate_up[i].astype(BF16), ffn1_w_down[i].astype(BF16))
        j = i // 2
        if i % 2 == 0:
            w_in = attn_w_in[j].astype(BF16)
            wqt = w_in[:, :qk_width].T
            wk = w_in[:, qk_width:2 * qk_width]
            wvt = w_in[:, 2 * qk_width:].T
            feat_gain = attn_q_norm[j].astype(F32) * attn_k_norm[j].astype(F32) * (hd ** -0.5 * LOG2_E)
            kgain = jnp.tile(feat_gain, 2 * h).reshape(1, -1)
            score_bound = 1.01 * hd * jnp.max(jnp.abs(feat_gain))
            stable = (score_bound > SCORE_BOUND_LOG2).astype(jnp.int32).reshape(1)
            k, qt, vt = _attn_in(xf, _row(mix_norm[i]), wk, wqt, wvt, kgain, hd)
            o = _flash(stable, qt.reshape(b, nq, d, ATTN_TILE), k.reshape(b, s, d), vt.reshape(b, nq, d, ATTN_TILE),
                       _row(attn_lambda_q1[j]), _row(attn_lambda_k1[j]), _row(attn_lambda_q2[j]),
                       _row(attn_lambda_k2[j]), attn_subln[j].reshape(-1, 1).astype(F32), _lambda_init(i))
            xf = _attn_out(xf, o.reshape(b * s, d), attn_w_out[j].astype(BF16))
        else:
            half = gmlp_w_out.shape[1]
            gw = half // GMLP_GROUPS
            bs_full = jnp.repeat(gmlp_b_s[j].T, gw, axis=1).astype(F32)
            xf = _gmlp(xf, _row(mix_norm[i]), gmlp_w_in[j].astype(BF16), _row(gmlp_b_in[j]), _row(gmlp_ln_g[j]),
                       _row(gmlp_ln_b[j]), gmlp_w_s[j].astype(F32), bs_full, gmlp_w_out[j].astype(BF16),
                       _row(gmlp_b_out[j]))
        xf = _ffn(xf, _row(ffn2_norm[i]), ffn2_w_gate_up[i].astype(BF16), ffn2_w_down[i].astype(BF16))
    return xf.reshape(b, s, d)
```

```python
import functools
import math

import jax
import jax.numpy as jnp
from jax import lax
from jax.experimental import pallas as pl
from jax.experimental.pallas import tpu as pltpu

F32 = jnp.float32
BF16 = jnp.bfloat16

RMS_EPS = 1e-6
LN_EPS = 1e-5
DIFF_HEADS = 8
GMLP_GROUPS = 8
GMLP_CHUNK = 128

MXU_TILE = 256
VMEM_LIMIT_BYTES = 56 * 1024 * 1024
NEG_BIG = -1e30
LOG2_E = math.log2(math.e)
SCORE_BOUND_LOG2 = 60.0

TOKEN_TILE = 512
ATTN_TILE = 512
Q_TILES = 4
KV_BLOCKS_PER_STEP = 4


def _const_spec(shape):
    return pl.BlockSpec(shape, lambda *_: (0,) * len(shape), pipeline_mode=pl.Buffered(1))


def _params(n_axes):
    return pltpu.CompilerParams(
        dimension_semantics=("arbitrary",) * n_axes,
        vmem_limit_bytes=VMEM_LIMIT_BYTES,
    )


def _rms_rows(x, gain):
    ms = jnp.mean(x * x, axis=-1, keepdims=True)
    return x * lax.rsqrt(ms + RMS_EPS) * gain


def _ffn_kernel(x_ref, g_ref, wgu_ref, wd_ref, o_ref, h_ref, *, d_ff):
    xn = _rms_rows(x_ref[...], g_ref[...]).astype(BF16)
    for j in range(d_ff // MXU_TILE):
        lo = j * MXU_TILE
        gate = jnp.dot(xn, wgu_ref[:, lo:lo + MXU_TILE], preferred_element_type=F32)
        up = jnp.dot(xn, wgu_ref[:, d_ff + lo:d_ff + lo + MXU_TILE], preferred_element_type=F32)
        h_ref[:, lo:lo + MXU_TILE] = (gate * jax.nn.sigmoid(gate) * up).astype(BF16)
    y = jnp.dot(h_ref[...], wd_ref[...], preferred_element_type=F32)
    o_ref[...] = x_ref[...] + 0.5 * y


def _ffn(x, gain, w_gate_up, w_down):
    t, d = x.shape
    d_ff = w_down.shape[0]
    assert d_ff % MXU_TILE == 0 and t % TOKEN_TILE == 0
    tm = TOKEN_TILE
    return pl.pallas_call(
        functools.partial(_ffn_kernel, d_ff=d_ff),
        out_shape=jax.ShapeDtypeStruct((t, d), F32),
        grid=(t // tm,),
        in_specs=[
            pl.BlockSpec((tm, d), lambda i: (i, 0)),
            _const_spec((1, d)),
            _const_spec((d, 2 * d_ff)),
            _const_spec((d_ff, d)),
        ],
        out_specs=pl.BlockSpec((tm, d), lambda i: (i, 0)),
        scratch_shapes=[pltpu.VMEM((tm, d_ff), BF16)],
        compiler_params=_params(1),
        name="ffn",
    )(x, gain, w_gate_up, w_down)


def _group_ones(n, group):
    r = lax.broadcasted_iota(jnp.int32, (n, n), 0) // group
    c = lax.broadcasted_iota(jnp.int32, (n, n), 1) // group
    return (r == c).astype(BF16)


def _attn_in_kernel(x_ref, g_ref, wk_ref, wqt_ref, wvt_ref, kgain_ref, k_ref, qt_ref, vt_ref, *, hd):
    d = x_ref.shape[1]
    xn = _rms_rows(x_ref[...], g_ref[...]).astype(BF16)
    ones = _group_ones(MXU_TILE, hd)
    nt_dims = (((1,), (1,)), ((), ()))

    k = jnp.dot(xn, wk_ref[...], preferred_element_type=F32)
    for j in range(d // MXU_TILE):
        lo = j * MXU_TILE
        kj = k[:, lo:lo + MXU_TILE]
        ss = jnp.dot((kj * kj).astype(BF16), ones, preferred_element_type=F32)
        kn = kj * lax.rsqrt(ss * (1.0 / hd) + RMS_EPS) * kgain_ref[:, lo:lo + MXU_TILE]
        k_ref[:, lo:lo + MXU_TILE] = kn.astype(BF16)

    qt = lax.dot_general(wqt_ref[...], xn, nt_dims, preferred_element_type=F32)
    for j in range(d // MXU_TILE):
        lo = j * MXU_TILE
        qj = qt[lo:lo + MXU_TILE, :]
        ss = jnp.dot(ones, (qj * qj).astype(BF16), preferred_element_type=F32)
        qt_ref[0, lo:lo + MXU_TILE, :] = (qj * lax.rsqrt(ss * (1.0 / hd) + RMS_EPS)).astype(BF16)

    vt = lax.dot_general(wvt_ref[...], xn, nt_dims, preferred_element_type=F32)
    vt_ref[0] = vt.astype(BF16)


def _attn_in(x, gain, wk, wqt, wvt, kgain, hd):
    t, d = x.shape
    tm = ATTN_TILE
    nt = t // tm
    return pl.pallas_call(
        functools.partial(_attn_in_kernel, hd=hd),
        out_shape=(
            jax.ShapeDtypeStruct((t, d), BF16),
            jax.ShapeDtypeStruct((nt, d, tm), BF16),
            jax.ShapeDtypeStruct((nt, d, tm), BF16),
        ),
        grid=(nt,),
        in_specs=[
            pl.BlockSpec((tm, d), lambda i: (i, 0)),
            _const_spec((1, d)),
            _const_spec((d, d)),
            _const_spec((d, d)),
            _const_spec((d, d)),
            _const_spec((1, d)),
        ],
        out_specs=(
            pl.BlockSpec((tm, d), lambda i: (i, 0)),
            pl.BlockSpec((1, d, tm), lambda i: (i, 0, 0)),
            pl.BlockSpec((1, d, tm), lambda i: (i, 0, 0)),
        ),
        compiler_params=_params(1),
        name="attn_in",
    )(x, gain, wk, wqt, wvt, kgain)


def _flash_kernel(stable_ref, qt_ref, k_ref, vt_ref, lq1_ref, lk1_ref, lq2_ref, lk2_ref, sub_ref, o_ref,
                  q_scr, m_scr, l_scr, acc_scr, *, hd, lam_init):
    i = pl.program_id(2)
    tk = vt_ref.shape[3]
    tq = Q_TILES * tk

    qt = jnp.concatenate([qt_ref[0, n] for n in range(Q_TILES)], axis=1)
    row = lax.broadcasted_iota(jnp.int32, qt.shape, 0)
    zero = jnp.zeros_like(qt)
    q_scr[0] = jnp.where(row < hd, qt, zero)
    q_scr[1] = jnp.where(row >= hd, qt, zero)
    m_scr[...] = jnp.full(m_scr.shape, NEG_BIG, F32)
    l_scr[...] = jnp.zeros(l_scr.shape, F32)
    acc_scr[...] = jnp.zeros(acc_scr.shape, F32)

    def kv_block(j, nblk, q_lo, diagonal, stable):
        start = pl.multiple_of(j * tk, tk)
        kj = k_ref[0, pl.ds(start, nblk * tk), :]
        vj = jnp.concatenate([vt_ref[0, j + n] for n in range(nblk)], axis=1)
        for c in range(2):
            s = jnp.dot(kj, q_scr[c, :, q_lo:], preferred_element_type=F32)
            if diagonal:
                kpos = lax.broadcasted_iota(jnp.int32, s.shape, 0) - (nblk - 1) * tk
                qpos = lax.broadcasted_iota(jnp.int32, s.shape, 1)
                s = jnp.where(kpos <= qpos, s, NEG_BIG)
            if stable:
                m_old = m_scr[c, :, q_lo:]
                m_new = jnp.maximum(m_old, jnp.max(s, axis=0, keepdims=True))
                alpha = jnp.exp2(m_old - m_new)
                p = jnp.exp2(s - m_new)
                l_scr[c, :, q_lo:] = alpha * l_scr[c, :, q_lo:] + jnp.sum(p, axis=0, keepdims=True)
                acc_scr[c, :, q_lo:] = (alpha * acc_scr[c, :, q_lo:]
                                        + jnp.dot(vj, p.astype(BF16), preferred_element_type=F32))
                m_scr[c, :, q_lo:] = m_new
            else:
                p = jnp.exp2(s)
                l_scr[c, :, q_lo:] = l_scr[c, :, q_lo:] + jnp.sum(p, axis=0, keepdims=True)
                acc_scr[c, :, q_lo:] = acc_scr[c, :, q_lo:] + jnp.dot(vj, p.astype(BF16), preferred_element_type=F32)

    def run(stable, chunk):
        n_below = Q_TILES * i
        n_trips = n_below // chunk

        def body(jj, carry):
            kv_block(jj * chunk, chunk, 0, diagonal=False, stable=stable)
            return carry
        lax.fori_loop(0, n_trips, body, 0)

        for rem in range(0, chunk, Q_TILES):
            @pl.when(n_below - n_trips * chunk == rem)
            def _():
                kv_block(n_trips * chunk, rem + 1, 0, diagonal=True, stable=stable)
                for n in range(1, Q_TILES):
                    kv_block(n_below + n, 1, n * tk, diagonal=True, stable=stable)

    @pl.when(stable_ref[0] == 0)
    def _():
        run(stable=False, chunk=KV_BLOCKS_PER_STEP)

    @pl.when(stable_ref[0] != 0)
    def _():
        run(stable=True, chunk=1)

    lam = (jnp.exp(jnp.sum(lq1_ref[...] * lk1_ref[...], axis=-1, keepdims=True))
           - jnp.exp(jnp.sum(lq2_ref[...] * lk2_ref[...], axis=-1, keepdims=True))
           + lam_init)
    o = acc_scr[0] * (1.0 / l_scr[0]) - acc_scr[1] * (lam / l_scr[1])
    ms = jnp.mean(o * o, axis=0, keepdims=True)
    on = o * lax.rsqrt(ms + RMS_EPS) * (sub_ref[...] * (1.0 - lam_init))
    o_ref[0] = on.T.astype(BF16)


def _flash(stable, qt, k, vt, lq1, lk1, lq2, lk2, subln_col, lam_init):
    b, nk, d, tk = vt.shape
    _, s, _ = k.shape
    assert qt.shape == vt.shape and nk % Q_TILES == 0 and KV_BLOCKS_PER_STEP % Q_TILES == 0
    nq = nk // Q_TILES
    tq = Q_TILES * tk
    h = DIFF_HEADS
    hw = d // h
    hd = hw // 2
    vec = pl.BlockSpec((1, hd), lambda bb, hh, ii, st: (0, 0))
    grid_spec = pltpu.PrefetchScalarGridSpec(
        num_scalar_prefetch=1,
        grid=(b, h, nq),
        in_specs=[
            pl.BlockSpec((1, Q_TILES, hw, tk), lambda bb, hh, ii, st: (bb, ii, hh, 0)),
            pl.BlockSpec((1, s, hw), lambda bb, hh, ii, st: (bb, 0, hh)),
            pl.BlockSpec((1, nk, hw, tk), lambda bb, hh, ii, st: (bb, 0, hh, 0)),
            vec, vec, vec, vec,
            pl.BlockSpec((hw, 1), lambda bb, hh, ii, st: (0, 0)),
        ],
        out_specs=pl.BlockSpec((1, tq, hw), lambda bb, hh, ii, st: (bb, ii, hh)),
        scratch_shapes=[
            pltpu.VMEM((2, hw, tq), BF16),
            pltpu.VMEM((2, 1, tq), F32),
            pltpu.VMEM((2, 1, tq), F32),
            pltpu.VMEM((2, hw, tq), F32),
        ],
    )
    return pl.pallas_call(
        functools.partial(_flash_kernel, hd=hd, lam_init=lam_init),
        out_shape=jax.ShapeDtypeStruct((b, s, d), BF16),
        grid_spec=grid_spec,
        compiler_params=_params(3),
        name="flash",
    )(stable, qt, k, vt, lq1, lk1, lq2, lk2, subln_col)


def _attn_out_kernel(x_ref, o_ref, w_ref, y_ref):
    y_ref[...] = x_ref[...] + jnp.dot(o_ref[...], w_ref[...], preferred_element_type=F32)


def _attn_out(x, o, w_out):
    t, d = x.shape
    tm = TOKEN_TILE
    return pl.pallas_call(
        _attn_out_kernel,
        out_shape=jax.ShapeDtypeStruct((t, d), F32),
        grid=(t // tm,),
        in_specs=[
            pl.BlockSpec((tm, d), lambda i: (i, 0)),
            pl.BlockSpec((tm, d), lambda i: (i, 0)),
            _const_spec((d, d)),
        ],
        out_specs=pl.BlockSpec((tm, d), lambda i: (i, 0)),
        compiler_params=_params(1),
        name="attn_out",
    )(x, o, w_out)


def _gelu(z):
    return 0.5 * z * (1.0 + lax.erf(z * math.sqrt(0.5)))


def _gmlp_kernel(x_ref, g_ref, win_ref, bin_ref, lng_ref, lnb_ref, ws_ref, bs_ref, wout_ref, bout_ref,
                 y_ref, v_scr, gated_scr, *, half):
    tm = x_ref.shape[0]
    gw = half // GMLP_GROUPS
    xn = _rms_rows(x_ref[...], g_ref[...]).astype(BF16)

    rsum = jnp.zeros((tm, 1), F32)
    rsq = jnp.zeros((tm, 1), F32)
    for g in range(GMLP_GROUPS):
        lo = half + g * gw
        z = jnp.dot(xn, win_ref[:, lo:lo + gw], preferred_element_type=F32) + bin_ref[:, lo:lo + gw]
        v = _gelu(z)
        v_scr[:, g * gw:(g + 1) * gw] = v
        rsum = rsum + jnp.sum(v, axis=-1, keepdims=True)
        rsq = rsq + jnp.sum(v * v, axis=-1, keepdims=True)
    mu = rsum * (1.0 / half)
    var = rsq * (1.0 / half) - mu * mu
    rstd = lax.rsqrt(var + LN_EPS)

    t_idx = lax.broadcasted_iota(jnp.int32, (GMLP_CHUNK, GMLP_CHUNK), 0)
    s_idx = lax.broadcasted_iota(jnp.int32, (GMLP_CHUNK, GMLP_CHUNK), 1)
    causal = s_idx <= t_idx

    for g in range(GMLP_GROUPS):
        lo = g * gw
        vn = ((v_scr[:, lo:lo + gw] - mu) * rstd * lng_ref[:, lo:lo + gw] + lnb_ref[:, lo:lo + gw]).astype(BF16)
        w = jnp.where(causal, ws_ref[g], 0.0).astype(BF16)
        z = jnp.dot(xn, win_ref[:, lo:lo + gw], preferred_element_type=F32) + bin_ref[:, lo:lo + gw]
        u = _gelu(z)
        for c in range(tm // GMLP_CHUNK):
            r = c * GMLP_CHUNK
            s = jnp.dot(w, vn[r:r + GMLP_CHUNK, :], preferred_element_type=F32) + bs_ref[:, lo:lo + gw]
            gated_scr[r:r + GMLP_CHUNK, lo:lo + gw] = (u[r:r + GMLP_CHUNK, :] * s).astype(BF16)

    y = jnp.dot(gated_scr[...], wout_ref[...], preferred_element_type=F32)
    y_ref[...] = x_ref[...] + y + bout_ref[...]


def _gmlp(x, gain, w_in, b_in, ln_g, ln_b, w_s, bs_full, w_out, b_out):
    t, d = x.shape
    half = w_out.shape[0]
    tm = TOKEN_TILE
    assert tm % GMLP_CHUNK == 0
    return pl.pallas_call(
        functools.partial(_gmlp_kernel, half=half),
        out_shape=jax.ShapeDtypeStruct((t, d), F32),
        grid=(t // tm,),
        in_specs=[
            pl.BlockSpec((tm, d), lambda i: (i, 0)),
            _const_spec((1, d)),
            _const_spec((d, 2 * half)),
            _const_spec((1, 2 * half)),
            _const_spec((1, half)),
            _const_spec((1, half)),
            _const_spec((GMLP_GROUPS, GMLP_CHUNK, GMLP_CHUNK)),
            _const_spec((GMLP_CHUNK, half)),
            _const_spec((half, d)),
            _const_spec((1, d)),
        ],
        out_specs=pl.BlockSpec((tm, d), lambda i: (i, 0)),
        scratch_shapes=[pltpu.VMEM((tm, half), F32), pltpu.VMEM((tm, half), BF16)],
        compiler_params=_params(1),
        name="gmlp",
    )(x, gain, w_in, b_in, ln_g, ln_b, w_s, bs_full, w_out, b_out)


def _lambda_init(layer_idx):
    return 0.8 - 0.6 * math.exp(-0.3 * layer_idx)


def _row(v):
    return v.reshape(1, -1).astype(F32)


def kernel(x, ffn1_norm, ffn1_w_gate_up, ffn1_w_down, mix_norm, ffn2_norm, ffn2_w_gate_up, ffn2_w_down, attn_w_in, attn_w_out, attn_q_norm, attn_k_norm, attn_lambda_q1, attn_lambda_k1, attn_lambda_q2, attn_lambda_k2, attn_subln, gmlp_w_in, gmlp_b_in, gmlp_ln_g, gmlp_ln_b, gmlp_w_s, gmlp_b_s, gmlp_w_out, gmlp_b_out):
    b, s, d = x.shape
    depth = ffn1_norm.shape[0]
    h = DIFF_HEADS
    hd = d // h // 2
    qk_width = h * 2 * hd
    nq = s // ATTN_TILE
    xf = x.reshape(b * s, d)

    for i in range(depth):
        xf = _ffn(xf, _row(ffn1_norm[i]), ffn1_w_gate_up[i].astype(BF16), ffn1_w_down[i].astype(BF16))
        j = i // 2
        if i % 2 == 0:
            w_in = attn_w_in[j].astype(BF16)
            wqt = w_in[:, :qk_width].T
            wk = w_in[:, qk_width:2 * qk_width]
            wvt = w_in[:, 2 * qk_width:].T
            feat_gain = attn_q_norm[j].astype(F32) * attn_k_norm[j].astype(F32) * (hd ** -0.5 * LOG2_E)
            kgain = jnp.tile(feat_gain, 2 * h).reshape(1, -1)
            score_bound = 1.01 * hd * jnp.max(jnp.abs(feat_gain))
            stable = (score_bound > SCORE_BOUND_LOG2).astype(jnp.int32).reshape(1)
            k, qt, vt = _attn_in(xf, _row(mix_norm[i]), wk, wqt, wvt, kgain, hd)
            o = _flash(stable, qt.reshape(b, nq, d, ATTN_TILE), k.reshape(b, s, d), vt.reshape(b, nq, d, ATTN_TILE),
                       _row(attn_lambda_q1[j]), _row(attn_lambda_k1[j]), _row(attn_lambda_q2[j]),
                       _row(attn_lambda_k2[j]), attn_subln[j].reshape(-1, 1).astype(F32), _lambda_init(i))
            xf = _attn_out(xf, o.reshape(b * s, d), attn_w_out[j].astype(BF16))
        else:
            half = gmlp_w_out.shape[1]
            gw = half // GMLP_GROUPS
            bs_full = jnp.repeat(gmlp_b_s[j].T, gw, axis=1).astype(F32)
            xf = _gmlp(xf, _row(mix_norm[i]), gmlp_w_in[j].astype(BF16), _row(gmlp_b_in[j]), _row(gmlp_ln_g[j]),
                       _row(gmlp_ln_b[j]), gmlp_w_s[j].astype(F32), bs_full, gmlp_w_out[j].astype(BF16),
                       _row(gmlp_b_out[j]))
        xf = _ffn(xf, _row(ffn2_norm[i]), ffn2_w_gate_up[i].astype(BF16), ffn2_w_down[i].astype(BF16))
    return xf.reshape(b, s, d)
```

```python
import functools
import math

import jax
import jax.numpy as jnp
from jax import lax
from jax.experimental import pallas as pl
from jax.experimental.pallas import tpu as pltpu

F32 = jnp.float32
BF16 = jnp.bfloat16

RMS_EPS = 1e-6
LN_EPS = 1e-5
DIFF_HEADS = 8
GMLP_GROUPS = 8
GMLP_CHUNK = 128

MXU_TILE = 256
VMEM_LIMIT_BYTES = 56 * 1024 * 1024
NEG_BIG = -1e30
LOG2_E = math.log2(math.e)
SCORE_BOUND_LOG2 = 60.0

TOKEN_TILE = 512
ATTN_TILE = 512
Q_TILES = 4
KV_BLOCKS_PER_STEP = 4


def _layer_spec(layer, shape, block_idx=None):
    block_idx = tuple(block_idx) if block_idx is not None else (0,) * len(shape)
    return pl.BlockSpec((None,) + tuple(shape), lambda *_: (layer,) + block_idx, pipeline_mode=pl.Buffered(1))


def _params(n_axes):
    return pltpu.CompilerParams(
        dimension_semantics=("arbitrary",) * n_axes,
        vmem_limit_bytes=VMEM_LIMIT_BYTES,
    )


def _rms_rows(x, gain):
    ms = jnp.mean(x * x, axis=-1, keepdims=True)
    return x * lax.rsqrt(ms + RMS_EPS) * gain


def _ffn_kernel(*refs, d_ff, with_proj):
    if with_proj:
        x_ref, a_ref, wp_ref, g_ref, wgu_ref, wd_ref, o_ref, h_ref = refs
        o_ref[...] = x_ref[...] + jnp.dot(a_ref[...], wp_ref[...], preferred_element_type=F32)
    else:
        x_ref, g_ref, wgu_ref, wd_ref, o_ref, h_ref = refs
        o_ref[...] = x_ref[...]
    xn = _rms_rows(o_ref[...], g_ref[...]).astype(BF16)
    for j in range(d_ff // MXU_TILE):
        lo = j * MXU_TILE
        gate = jnp.dot(xn, wgu_ref[:, lo:lo + MXU_TILE], preferred_element_type=F32)
        up = jnp.dot(xn, wgu_ref[:, d_ff + lo:d_ff + lo + MXU_TILE], preferred_element_type=F32)
        h_ref[:, lo:lo + MXU_TILE] = (gate * jax.nn.sigmoid(gate) * up).astype(BF16)
    y = jnp.dot(h_ref[...], wd_ref[...], preferred_element_type=F32)
    o_ref[...] = o_ref[...] + 0.5 * y


def _ffn(x, layer, gains, w_gate_up, w_down, proj=None):
    t, d = x.shape
    d_ff = w_down.shape[1]
    assert d_ff % MXU_TILE == 0 and t % TOKEN_TILE == 0
    tm = TOKEN_TILE
    row_spec = pl.BlockSpec((tm, d), lambda i: (i, 0))
    operands, in_specs = [x], [row_spec]
    if proj is not None:
        a, w_proj, proj_layer = proj
        operands += [a, w_proj]
        in_specs += [pl.BlockSpec((tm, a.shape[1]), lambda i: (i, 0)), _layer_spec(proj_layer, w_proj.shape[1:])]
    operands += [gains, w_gate_up, w_down]
    in_specs += [_layer_spec(layer, (1, d)), _layer_spec(layer, (d, 2 * d_ff)), _layer_spec(layer, (d_ff, d))]
    return pl.pallas_call(
        functools.partial(_ffn_kernel, d_ff=d_ff, with_proj=proj is not None),
        out_shape=jax.ShapeDtypeStruct((t, d), F32),
        grid=(t // tm,),
        in_specs=in_specs,
        out_specs=row_spec,
        scratch_shapes=[pltpu.VMEM((tm, d_ff), BF16)],
        compiler_params=_params(1),
        name="ffn",
    )(*operands)


def _group_ones(n, group):
    r = lax.broadcasted_iota(jnp.int32, (n, n), 0) // group
    c = lax.broadcasted_iota(jnp.int32, (n, n), 1) // group
    return (r == c).astype(BF16)


def _attn_in_kernel(x_ref, g_ref, wk_ref, wqt_ref, wvt_ref, kgain_ref, k_ref, qt_ref, vt_ref, *, hd):
    d = x_ref.shape[1]
    xn = _rms_rows(x_ref[...], g_ref[...]).astype(BF16)
    ones = _group_ones(MXU_TILE, hd)
    nt_dims = (((1,), (1,)), ((), ()))

    k = jnp.dot(xn, wk_ref[...], preferred_element_type=F32)
    for j in range(d // MXU_TILE):
        lo = j * MXU_TILE
        kj = k[:, lo:lo + MXU_TILE]
        ss = jnp.dot((kj * kj).astype(BF16), ones, preferred_element_type=F32)
        kn = kj * lax.rsqrt(ss * (1.0 / hd) + RMS_EPS) * kgain_ref[:, lo:lo + MXU_TILE]
        k_ref[:, lo:lo + MXU_TILE] = kn.astype(BF16)

    qt = lax.dot_general(wqt_ref[...], xn, nt_dims, preferred_element_type=F32)
    for j in range(d // MXU_TILE):
        lo = j * MXU_TILE
        qj = qt[lo:lo + MXU_TILE, :]
        ss = jnp.dot(ones, (qj * qj).astype(BF16), preferred_element_type=F32)
        qt_ref[0, lo:lo + MXU_TILE, :] = (qj * lax.rsqrt(ss * (1.0 / hd) + RMS_EPS)).astype(BF16)

    vt = lax.dot_general(wvt_ref[...], xn, nt_dims, preferred_element_type=F32)
    vt_ref[0] = vt.astype(BF16)


def _attn_in(x, layer, gains, attn_layer, w_in, w_in_t, kgain, hd):
    t, d = x.shape
    tm = ATTN_TILE
    nt = t // tm
    return pl.pallas_call(
        functools.partial(_attn_in_kernel, hd=hd),
        out_shape=(
            jax.ShapeDtypeStruct((t, d), BF16),
            jax.ShapeDtypeStruct((nt, d, tm), BF16),
            jax.ShapeDtypeStruct((nt, d, tm), BF16),
        ),
        grid=(nt,),
        in_specs=[
            pl.BlockSpec((tm, d), lambda i: (i, 0)),
            _layer_spec(layer, (1, d)),
            _layer_spec(attn_layer, (d, d), (0, 1)),
            _layer_spec(attn_layer, (d, d), (0, 0)),
            _layer_spec(attn_layer, (d, d), (2, 0)),
            _layer_spec(attn_layer, (1, d)),
        ],
        out_specs=(
            pl.BlockSpec((tm, d), lambda i: (i, 0)),
            pl.BlockSpec((1, d, tm), lambda i: (i, 0, 0)),
            pl.BlockSpec((1, d, tm), lambda i: (i, 0, 0)),
        ),
        compiler_params=_params(1),
        name="attn_in",
    )(x, gains, w_in, w_in_t, w_in_t, kgain)


def _flash_kernel(stable_ref, qt_ref, k_ref, vt_ref, lq1_ref, lk1_ref, lq2_ref, lk2_ref, sub_ref, o_ref,
                  q_scr, m_scr, l_scr, acc_scr, *, hd, lam_init):
    i = pl.program_id(2)
    tk = vt_ref.shape[3]
    tq = Q_TILES * tk

    qt = jnp.concatenate([qt_ref[0, n] for n in range(Q_TILES)], axis=1)
    row = lax.broadcasted_iota(jnp.int32, qt.shape, 0)
    zero = jnp.zeros_like(qt)
    q_scr[0] = jnp.where(row < hd, qt, zero)
    q_scr[1] = jnp.where(row >= hd, qt, zero)
    m_scr[...] = jnp.full(m_scr.shape, NEG_BIG, F32)
    l_scr[...] = jnp.zeros(l_scr.shape, F32)
    acc_scr[...] = jnp.zeros(acc_scr.shape, F32)

    def kv_block(j, nblk, q_lo, diagonal, stable):
        start = pl.multiple_of(j * tk, tk)
        kj = k_ref[0, pl.ds(start, nblk * tk), :]
        vj = jnp.concatenate([vt_ref[0, j + n] for n in range(nblk)], axis=1)
        for c in range(2):
            s = jnp.dot(kj, q_scr[c, :, q_lo:], preferred_element_type=F32)
            if diagonal:
                kpos = lax.broadcasted_iota(jnp.int32, s.shape, 0) - (nblk - 1) * tk
                qpos = lax.broadcasted_iota(jnp.int32, s.shape, 1)
                s = jnp.where(kpos <= qpos, s, NEG_BIG)
            if stable:
                m_old = m_scr[c, :, q_lo:]
                m_new = jnp.maximum(m_old, jnp.max(s, axis=0, keepdims=True))
                alpha = jnp.exp2(m_old - m_new)
                p = jnp.exp2(s - m_new)
                l_scr[c, :, q_lo:] = alpha * l_scr[c, :, q_lo:] + jnp.sum(p, axis=0, keepdims=True)
                acc_scr[c, :, q_lo:] = (alpha * acc_scr[c, :, q_lo:]
                                        + jnp.dot(vj, p.astype(BF16), preferred_element_type=F32))
                m_scr[c, :, q_lo:] = m_new
            else:
                p = jnp.exp2(s)
                l_scr[c, :, q_lo:] = l_scr[c, :, q_lo:] + jnp.sum(p, axis=0, keepdims=True)
                acc_scr[c, :, q_lo:] = acc_scr[c, :, q_lo:] + jnp.dot(vj, p.astype(BF16), preferred_element_type=F32)

    def run(stable, chunk):
        n_below = Q_TILES * i
        n_trips = n_below // chunk

        def body(jj, carry):
            kv_block(jj * chunk, chunk, 0, diagonal=False, stable=stable)
            return carry
        lax.fori_loop(0, n_trips, body, 0)

        for rem in range(0, chunk, Q_TILES):
            @pl.when(n_below - n_trips * chunk == rem)
            def _():
                kv_block(n_trips * chunk, rem + 1, 0, diagonal=True, stable=stable)
                for n in range(1, Q_TILES):
                    kv_block(n_below + n, 1, n * tk, diagonal=True, stable=stable)

    @pl.when(stable_ref[0] == 0)
    def _():
        run(stable=False, chunk=KV_BLOCKS_PER_STEP)

    @pl.when(stable_ref[0] != 0)
    def _():
        run(stable=True, chunk=1)

    lam = (jnp.exp(jnp.sum(lq1_ref[...] * lk1_ref[...], axis=-1, keepdims=True))
           - jnp.exp(jnp.sum(lq2_ref[...] * lk2_ref[...], axis=-1, keepdims=True))
           + lam_init)
    o = acc_scr[0] * (1.0 / l_scr[0]) - acc_scr[1] * (lam / l_scr[1])
    ms = jnp.mean(o * o, axis=0, keepdims=True)
    on = o * lax.rsqrt(ms + RMS_EPS) * (sub_ref[...] * (1.0 - lam_init))
    o_ref[0] = on.T.astype(BF16)


def _flash(stable, attn_layer, qt, k, vt, lq1, lk1, lq2, lk2, subln_col, lam_init):
    b, nk, d, tk = vt.shape
    _, s, _ = k.shape
    assert qt.shape == vt.shape and nk % Q_TILES == 0 and KV_BLOCKS_PER_STEP % Q_TILES == 0
    nq = nk // Q_TILES
    tq = Q_TILES * tk
    h = DIFF_HEADS
    hw = d // h
    hd = hw // 2
    vec = pl.BlockSpec((None, 1, hd), lambda bb, hh, ii, st: (attn_layer, 0, 0))
    grid_spec = pltpu.PrefetchScalarGridSpec(
        num_scalar_prefetch=1,
        grid=(b, h, nq),
        in_specs=[
            pl.BlockSpec((1, Q_TILES, hw, tk), lambda bb, hh, ii, st: (bb, ii, hh, 0)),
            pl.BlockSpec((1, s, hw), lambda bb, hh, ii, st: (bb, 0, hh)),
            pl.BlockSpec((1, nk, hw, tk), lambda bb, hh, ii, st: (bb, 0, hh, 0)),
            vec, vec, vec, vec,
            pl.BlockSpec((None, hw, 1), lambda bb, hh, ii, st: (attn_layer, 0, 0)),
        ],
        out_specs=pl.BlockSpec((1, tq, hw), lambda bb, hh, ii, st: (bb, ii, hh)),
        scratch_shapes=[
            pltpu.VMEM((2, hw, tq), BF16),
            pltpu.VMEM((2, 1, tq), F32),
            pltpu.VMEM((2, 1, tq), F32),
            pltpu.VMEM((2, hw, tq), F32),
        ],
    )
    return pl.pallas_call(
        functools.partial(_flash_kernel, hd=hd, lam_init=lam_init),
        out_shape=jax.ShapeDtypeStruct((b, s, d), BF16),
        grid_spec=grid_spec,
        compiler_params=_params(3),
        name="flash",
    )(stable, qt, k, vt, lq1, lk1, lq2, lk2, subln_col)


def _gelu(z):
    return 0.5 * z * (1.0 + lax.erf(z * math.sqrt(0.5)))


def _gmlp_kernel(x_ref, g_ref, win_ref, bin_ref, lng_ref, lnb_ref, ws_ref, bs_ref, wout_ref, bout_ref,
                 y_ref, v_scr, gated_scr, *, half):
    tm = x_ref.shape[0]
    gw = half // GMLP_GROUPS
    xn = _rms_rows(x_ref[...], g_ref[...]).astype(BF16)

    rsum = jnp.zeros((tm, 1), F32)
    rsq = jnp.zeros((tm, 1), F32)
    for g in range(GMLP_GROUPS):
        lo = half + g * gw
        z = jnp.dot(xn, win_ref[:, lo:lo + gw], preferred_element_type=F32) + bin_ref[:, lo:lo + gw]
        v = _gelu(z)
        v_scr[:, g * gw:(g + 1) * gw] = v
        rsum = rsum + jnp.sum(v, axis=-1, keepdims=True)
        rsq = rsq + jnp.sum(v * v, axis=-1, keepdims=True)
    mu = rsum * (1.0 / half)
    var = rsq * (1.0 / half) - mu * mu
    rstd = lax.rsqrt(var + LN_EPS)

    t_idx = lax.broadcasted_iota(jnp.int32, (GMLP_CHUNK, GMLP_CHUNK), 0)
    s_idx = lax.broadcasted_iota(jnp.int32, (GMLP_CHUNK, GMLP_CHUNK), 1)
    causal = s_idx <= t_idx

    for g in range(GMLP_GROUPS):
        lo = g * gw
        vn = ((v_scr[:, lo:lo + gw] - mu) * rstd * lng_ref[:, lo:lo + gw] + lnb_ref[:, lo:lo + gw]).astype(BF16)
        w = jnp.where(causal, ws_ref[g], 0.0).astype(BF16)
        z = jnp.dot(xn, win_ref[:, lo:lo + gw], preferred_element_type=F32) + bin_ref[:, lo:lo + gw]
        u = _gelu(z)
        for c in range(tm // GMLP_CHUNK):
            r = c * GMLP_CHUNK
            s = jnp.dot(w, vn[r:r + GMLP_CHUNK, :], preferred_element_type=F32) + bs_ref[:, lo:lo + gw]
            gated_scr[r:r + GMLP_CHUNK, lo:lo + gw] = (u[r:r + GMLP_CHUNK, :] * s).astype(BF16)

    y = jnp.dot(gated_scr[...], wout_ref[...], preferred_element_type=F32)
    y_ref[...] = x_ref[...] + y + bout_ref[...]


def _gmlp(x, layer, gains, gl, w_in, b_in, ln_g, ln_b, w_s, bs_full, w_out, b_out):
    t, d = x.shape
    half = w_out.shape[1]
    tm = TOKEN_TILE
    assert tm % GMLP_CHUNK == 0
    return pl.pallas_call(
        functools.partial(_gmlp_kernel, half=half),
        out_shape=jax.ShapeDtypeStruct((t, d), F32),
        grid=(t // tm,),
        in_specs=[
            pl.BlockSpec((tm, d), lambda i: (i, 0)),
            _layer_spec(layer, (1, d)),
            _layer_spec(gl, (d, 2 * half)),
            _layer_spec(gl, (1, 2 * half)),
            _layer_spec(gl, (1, half)),
            _layer_spec(gl, (1, half)),
            _layer_spec(gl, (GMLP_GROUPS, GMLP_CHUNK, GMLP_CHUNK)),
            _layer_spec(gl, (GMLP_CHUNK, half)),
            _layer_spec(gl, (half, d)),
            _layer_spec(gl, (1, d)),
        ],
        out_specs=pl.BlockSpec((tm, d), lambda i: (i, 0)),
        scratch_shapes=[pltpu.VMEM((tm, half), F32), pltpu.VMEM((tm, half), BF16)],
        compiler_params=_params(1),
        name="gmlp",
    )(x, gains, w_in, b_in, ln_g, ln_b, w_s, bs_full, w_out, b_out)


def _lambda_init(layer_idx):
    return 0.8 - 0.6 * math.exp(-0.3 * layer_idx)


def _rows(v):
    return v.reshape(v.shape[0], 1, v.shape[1]).astype(F32)


def kernel(x, ffn1_norm, ffn1_w_gate_up, ffn1_w_down, mix_norm, ffn2_norm, ffn2_w_gate_up, ffn2_w_down, attn_w_in, attn_w_out, attn_q_norm, attn_k_norm, attn_lambda_q1, attn_lambda_k1, attn_lambda_q2, attn_lambda_k2, attn_subln, gmlp_w_in, gmlp_b_in, gmlp_ln_g, gmlp_ln_b, gmlp_w_s, gmlp_b_s, gmlp_w_out, gmlp_b_out):
    b, s, d = x.shape
    depth = ffn1_norm.shape[0]
    h = DIFF_HEADS
    hd = d // h // 2
    nq = s // ATTN_TILE
    xf = x.reshape(b * s, d)

    ffn1 = (_rows(ffn1_norm), ffn1_w_gate_up.astype(BF16), ffn1_w_down.astype(BF16))
    ffn2 = (_rows(ffn2_norm), ffn2_w_gate_up.astype(BF16), ffn2_w_down.astype(BF16))
    mix_gains = _rows(mix_norm)
    a_w_in = attn_w_in.astype(BF16)
    a_w_in_t = jnp.swapaxes(a_w_in, 1, 2)
    a_w_out = attn_w_out.astype(BF16)
    feat_gain = attn_q_norm.astype(F32) * attn_k_norm.astype(F32) * (hd ** -0.5 * LOG2_E)
    kgain = _rows(jnp.tile(feat_gain, (1, 2 * h)))
    score_bound = 1.01 * hd * jnp.max(jnp.abs(feat_gain), axis=1)
    stable = (score_bound > SCORE_BOUND_LOG2).astype(jnp.int32)
    lam_vecs = [_rows(v) for v in (attn_lambda_q1, attn_lambda_k1, attn_lambda_q2, attn_lambda_k2)]
    subln_col = attn_subln.reshape(attn_subln.shape[0], -1, 1).astype(F32)
    half = gmlp_w_out.shape[1]
    g_params = (gmlp_w_in.astype(BF16), _rows(gmlp_b_in), _rows(gmlp_ln_g), _rows(gmlp_ln_b), gmlp_w_s.astype(F32),
                jnp.repeat(jnp.swapaxes(gmlp_b_s, 1, 2), half // GMLP_GROUPS, axis=2).astype(F32),
                gmlp_w_out.astype(BF16), _rows(gmlp_b_out))

    proj = None
    for i in range(depth):
        xf = _ffn(xf, i, *ffn1)
        j = i // 2
        if i % 2 == 0:
            k, qt, vt = _attn_in(xf, i, mix_gains, j, a_w_in, a_w_in_t, kgain, hd)
            o = _flash(stable[j:j + 1], j, qt.reshape(b, nq, d, ATTN_TILE), k.reshape(b, s, d),
                       vt.reshape(b, nq, d, ATTN_TILE), *lam_vecs, subln_col, _lambda_init(i))
            proj = (o.reshape(b * s, d), a_w_out, j)
        else:
            xf = _gmlp(xf, i, mix_gains, j, *g_params)
            proj = None
        xf = _ffn(xf, i, *ffn2, proj=proj)
    return xf.reshape(b, s, d)
```

```python
import functools
import math

import jax
import jax.numpy as jnp
from jax import lax
from jax.experimental import pallas as pl
from jax.experimental.pallas import tpu as pltpu

F32 = jnp.float32
BF16 = jnp.bfloat16

RMS_EPS = 1e-6
LN_EPS = 1e-5
DIFF_HEADS = 8
GMLP_GROUPS = 8
GMLP_CHUNK = 128

MXU_TILE = 256
VMEM_LIMIT_BYTES = 56 * 1024 * 1024
NEG_BIG = -1e30
LOG2_E = math.log2(math.e)
SCORE_BOUND_LOG2 = 60.0

TOKEN_TILE = 512
FFN_TILE = 1024
ATTN_TILE = 512
Q_TILES = 4
KV_BLOCKS_PER_STEP = 4


def _layer_spec(layer, shape, block_idx=None):
    block_idx = tuple(block_idx) if block_idx is not None else (0,) * len(shape)
    return pl.BlockSpec((None,) + tuple(shape), lambda *_: (layer,) + block_idx, pipeline_mode=pl.Buffered(1))


def _params(n_axes):
    return pltpu.CompilerParams(
        dimension_semantics=("arbitrary",) * n_axes,
        vmem_limit_bytes=VMEM_LIMIT_BYTES,
    )


def _rms_rows(x, gain):
    ms = jnp.mean(x * x, axis=-1, keepdims=True)
    return x * lax.rsqrt(ms + RMS_EPS) * gain


def _ffn_kernel(*refs, d_ff, with_proj, n_cast):
    refs = list(refs)
    x_ref = refs.pop(0)
    a_ref, wp_ref = (refs.pop(0), refs.pop(0)) if with_proj else (None, None)
    g_ref, wgu_ref, wd_ref = refs.pop(0), refs.pop(0), refs.pop(0)
    cast_src = [refs.pop(0) for _ in range(n_cast)]
    o_ref = refs.pop(0)
    cast_dst = [refs.pop(0) for _ in range(n_cast)]
    (h_ref,) = refs
    for src, dst in zip(cast_src, cast_dst):
        dst[...] = src[...].astype(BF16)
    if with_proj:
        o_ref[...] = x_ref[...] + jnp.dot(a_ref[...], wp_ref[...], preferred_element_type=F32)
    else:
        o_ref[...] = x_ref[...]
    xn = _rms_rows(o_ref[...], g_ref[...]).astype(BF16)
    for j in range(d_ff // MXU_TILE):
        lo = j * MXU_TILE
        gate = jnp.dot(xn, wgu_ref[:, lo:lo + MXU_TILE], preferred_element_type=F32)
        up = jnp.dot(xn, wgu_ref[:, d_ff + lo:d_ff + lo + MXU_TILE], preferred_element_type=F32)
        h_ref[:, lo:lo + MXU_TILE] = (gate * jax.nn.sigmoid(gate) * up).astype(BF16)
    y = jnp.dot(h_ref[...], wd_ref[...], preferred_element_type=F32)
    o_ref[...] = o_ref[...] + 0.5 * y


def _ffn(x, layer, gains, w_gate_up, w_down, proj=None, cast=(), w_layer=0):
    t, d = x.shape
    d_ff = w_down.shape[1]
    assert d_ff % MXU_TILE == 0 and t % FFN_TILE == 0
    tm = FFN_TILE
    steps = t // tm
    row_spec = pl.BlockSpec((tm, d), lambda i: (i, 0))
    operands, in_specs = [x], [row_spec]
    if proj is not None:
        a, w_proj, proj_layer = proj
        operands += [a, w_proj]
        in_specs += [pl.BlockSpec((tm, a.shape[1]), lambda i: (i, 0)), _layer_spec(proj_layer, w_proj.shape[1:])]
    operands += [gains, w_gate_up, w_down]
    in_specs += [_layer_spec(layer, (1, d)), _layer_spec(w_layer, (d, 2 * d_ff)), _layer_spec(w_layer, (d_ff, d))]
    out_shape, out_specs = [jax.ShapeDtypeStruct((t, d), F32)], [row_spec]
    for stack, src_layer in cast:
        n_layers, rows, cols = stack.shape
        assert rows % steps == 0
        chunk = rows // steps
        operands.append(stack.reshape(n_layers, steps, chunk, cols))
        in_specs.append(pl.BlockSpec((None, None, chunk, cols), lambda i, src_layer=src_layer: (src_layer, i, 0, 0)))
        out_shape.append(jax.ShapeDtypeStruct((steps, chunk, cols), BF16))
        out_specs.append(pl.BlockSpec((None, chunk, cols), lambda i: (i, 0, 0)))
    outs = pl.pallas_call(
        functools.partial(_ffn_kernel, d_ff=d_ff, with_proj=proj is not None, n_cast=len(cast)),
        out_shape=out_shape,
        grid=(steps,),
        in_specs=in_specs,
        out_specs=out_specs,
        scratch_shapes=[pltpu.VMEM((tm, d_ff), BF16)],
        compiler_params=_params(1),
        name="ffn",
    )(*operands)
    return [outs[0]] + [o.reshape(1, stack.shape[1], stack.shape[2]) for o, (stack, _) in zip(outs[1:], cast)]


def _group_ones(n, group):
    r = lax.broadcasted_iota(jnp.int32, (n, n), 0) // group
    c = lax.broadcasted_iota(jnp.int32, (n, n), 1) // group
    return (r == c).astype(BF16)


def _attn_in_kernel(x_ref, g_ref, wk_ref, wqt_ref, wvt_ref, kgain_ref, k_ref, qt_ref, vt_ref, *, hd):
    d = x_ref.shape[1]
    xn = _rms_rows(x_ref[...], g_ref[...]).astype(BF16)
    ones = _group_ones(MXU_TILE, hd)
    nt_dims = (((1,), (1,)), ((), ()))

    k = jnp.dot(xn, wk_ref[...], preferred_element_type=F32)
    for j in range(d // MXU_TILE):
        lo = j * MXU_TILE
        kj = k[:, lo:lo + MXU_TILE]
        ss = jnp.dot((kj * kj).astype(BF16), ones, preferred_element_type=F32)
        kn = kj * lax.rsqrt(ss * (1.0 / hd) + RMS_EPS) * kgain_ref[:, lo:lo + MXU_TILE]
        k_ref[:, lo:lo + MXU_TILE] = kn.astype(BF16)

    qt = lax.dot_general(wqt_ref[...], xn, nt_dims, preferred_element_type=F32)
    for j in range(d // MXU_TILE):
        lo = j * MXU_TILE
        qj = qt[lo:lo + MXU_TILE, :]
        ss = jnp.dot(ones, (qj * qj).astype(BF16), preferred_element_type=F32)
        qt_ref[0, lo:lo + MXU_TILE, :] = (qj * lax.rsqrt(ss * (1.0 / hd) + RMS_EPS)).astype(BF16)

    vt = lax.dot_general(wvt_ref[...], xn, nt_dims, preferred_element_type=F32)
    vt_ref[0] = vt.astype(BF16)


def _attn_in(x, layer, gains, attn_layer, w_in, w_in_t, kgain, hd):
    t, d = x.shape
    tm = ATTN_TILE
    nt = t // tm
    return pl.pallas_call(
        functools.partial(_attn_in_kernel, hd=hd),
        out_shape=(
            jax.ShapeDtypeStruct((t, d), BF16),
            jax.ShapeDtypeStruct((nt, d, tm), BF16),
            jax.ShapeDtypeStruct((nt, d, tm), BF16),
        ),
        grid=(nt,),
        in_specs=[
            pl.BlockSpec((tm, d), lambda i: (i, 0)),
            _layer_spec(layer, (1, d)),
            _layer_spec(attn_layer, (d, d), (0, 1)),
            _layer_spec(attn_layer, (d, d), (0, 0)),
            _layer_spec(attn_layer, (d, d), (2, 0)),
            _layer_spec(attn_layer, (1, d)),
        ],
        out_specs=(
            pl.BlockSpec((tm, d), lambda i: (i, 0)),
            pl.BlockSpec((1, d, tm), lambda i: (i, 0, 0)),
            pl.BlockSpec((1, d, tm), lambda i: (i, 0, 0)),
        ),
        compiler_params=_params(1),
        name="attn_in",
    )(x, gains, w_in, w_in_t, w_in_t, kgain)


def _flash_kernel(stable_ref, qt_ref, k_ref, vt_ref, lq1_ref, lk1_ref, lq2_ref, lk2_ref, sub_ref, o_ref,
                  q_scr, m_scr, l_scr, acc_scr, *, hd, lam_init):
    i = pl.program_id(2)
    tk = vt_ref.shape[3]
    tq = Q_TILES * tk

    qt = jnp.concatenate([qt_ref[0, n] for n in range(Q_TILES)], axis=1)
    row = lax.broadcasted_iota(jnp.int32, qt.shape, 0)
    zero = jnp.zeros_like(qt)
    q_scr[0] = jnp.where(row < hd, qt, zero)
    q_scr[1] = jnp.where(row >= hd, qt, zero)
    m_scr[...] = jnp.full(m_scr.shape, NEG_BIG, F32)
    l_scr[...] = jnp.zeros(l_scr.shape, F32)
    acc_scr[...] = jnp.zeros(acc_scr.shape, F32)

    def kv_block(j, nblk, q_lo, diagonal, stable):
        start = pl.multiple_of(j * tk, tk)
        kj = k_ref[0, pl.ds(start, nblk * tk), :]
        vj = jnp.concatenate([vt_ref[0, j + n] for n in range(nblk)], axis=1)
        for c in range(2):
            s = jnp.dot(kj, q_scr[c, :, q_lo:], preferred_element_type=F32)
            if diagonal:
                kpos = lax.broadcasted_iota(jnp.int32, s.shape, 0) - (nblk - 1) * tk
                qpos = lax.broadcasted_iota(jnp.int32, s.shape, 1)
                s = jnp.where(kpos <= qpos, s, NEG_BIG)
            if stable:
                m_old = m_scr[c, :, q_lo:]
                m_new = jnp.maximum(m_old, jnp.max(s, axis=0, keepdims=True))
                alpha = jnp.exp2(m_old - m_new)
                p = jnp.exp2(s - m_new)
                l_scr[c, :, q_lo:] = alpha * l_scr[c, :, q_lo:] + jnp.sum(p, axis=0, keepdims=True)
                acc_scr[c, :, q_lo:] = (alpha * acc_scr[c, :, q_lo:]
                                        + jnp.dot(vj, p.astype(BF16), preferred_element_type=F32))
                m_scr[c, :, q_lo:] = m_new
            else:
                p = jnp.exp2(s)
                l_scr[c, :, q_lo:] = l_scr[c, :, q_lo:] + jnp.sum(p, axis=0, keepdims=True)
                acc_scr[c, :, q_lo:] = acc_scr[c, :, q_lo:] + jnp.dot(vj, p.astype(BF16), preferred_element_type=F32)

    def run(stable, chunk):
        n_below = Q_TILES * i
        n_trips = n_below // chunk

        def body(jj, carry):
            kv_block(jj * chunk, chunk, 0, diagonal=False, stable=stable)
            return carry
        lax.fori_loop(0, n_trips, body, 0)

        for rem in range(0, chunk, Q_TILES):
            @pl.when(n_below - n_trips * chunk == rem)
            def _():
                kv_block(n_trips * chunk, rem + 1, 0, diagonal=True, stable=stable)
                for n in range(1, Q_TILES):
                    kv_block(n_below + n, 1, n * tk, diagonal=True, stable=stable)

    @pl.when(stable_ref[0] == 0)
    def _():
        run(stable=False, chunk=KV_BLOCKS_PER_STEP)

    @pl.when(stable_ref[0] != 0)
    def _():
        run(stable=True, chunk=1)

    lam = (jnp.exp(jnp.sum(lq1_ref[...] * lk1_ref[...], axis=-1, keepdims=True))
           - jnp.exp(jnp.sum(lq2_ref[...] * lk2_ref[...], axis=-1, keepdims=True))
           + lam_init)
    o = acc_scr[0] * (1.0 / l_scr[0]) - acc_scr[1] * (lam / l_scr[1])
    ms = jnp.mean(o * o, axis=0, keepdims=True)
    on = o * lax.rsqrt(ms + RMS_EPS) * (sub_ref[...] * (1.0 - lam_init))
    o_ref[0] = on.T.astype(BF16)


def _flash(stable, attn_layer, qt, k, vt, lq1, lk1, lq2, lk2, subln_col, lam_init):
    b, nk, d, tk = vt.shape
    _, s, _ = k.shape
    assert qt.shape == vt.shape and nk % Q_TILES == 0 and KV_BLOCKS_PER_STEP % Q_TILES == 0
    nq = nk // Q_TILES
    tq = Q_TILES * tk
    h = DIFF_HEADS
    hw = d // h
    hd = hw // 2
    vec = pl.BlockSpec((None, 1, hd), lambda bb, hh, ii, st: (attn_layer, 0, 0))
    grid_spec = pltpu.PrefetchScalarGridSpec(
        num_scalar_prefetch=1,
        grid=(b, h, nq),
        in_specs=[
            pl.BlockSpec((1, Q_TILES, hw, tk), lambda bb, hh, ii, st: (bb, ii, hh, 0)),
            pl.BlockSpec((1, s, hw), lambda bb, hh, ii, st: (bb, 0, hh)),
            pl.BlockSpec((1, nk, hw, tk), lambda bb, hh, ii, st: (bb, 0, hh, 0)),
            vec, vec, vec, vec,
            pl.BlockSpec((None, hw, 1), lambda bb, hh, ii, st: (attn_layer, 0, 0)),
        ],
        out_specs=pl.BlockSpec((1, tq, hw), lambda bb, hh, ii, st: (bb, ii, hh)),
        scratch_shapes=[
            pltpu.VMEM((2, hw, tq), BF16),
            pltpu.VMEM((2, 1, tq), F32),
            pltpu.VMEM((2, 1, tq), F32),
            pltpu.VMEM((2, hw, tq), F32),
        ],
    )
    return pl.pallas_call(
        functools.partial(_flash_kernel, hd=hd, lam_init=lam_init),
        out_shape=jax.ShapeDtypeStruct((b, s, d), BF16),
        grid_spec=grid_spec,
        compiler_params=_params(3),
        name="flash",
    )(stable, qt, k, vt, lq1, lk1, lq2, lk2, subln_col)


def _gelu(z):
    return 0.5 * z * (1.0 + lax.erf(z * math.sqrt(0.5)))


def _gmlp_kernel(x_ref, g_ref, win_ref, bin_ref, lng_ref, lnb_ref, ws_ref, bs_ref, wout_ref, bout_ref,
                 y_ref, v_scr, gated_scr, *, half):
    tm = x_ref.shape[0]
    gw = half // GMLP_GROUPS
    xn = _rms_rows(x_ref[...], g_ref[...]).astype(BF16)

    rsum = jnp.zeros((tm, 1), F32)
    rsq = jnp.zeros((tm, 1), F32)
    for g in range(GMLP_GROUPS):
        lo = half + g * gw
        z = jnp.dot(xn, win_ref[:, lo:lo + gw], preferred_element_type=F32) + bin_ref[:, lo:lo + gw]
        v = _gelu(z)
        v_scr[:, g * gw:(g + 1) * gw] = v
        rsum = rsum + jnp.sum(v, axis=-1, keepdims=True)
        rsq = rsq + jnp.sum(v * v, axis=-1, keepdims=True)
    mu = rsum * (1.0 / half)
    var = rsq * (1.0 / half) - mu * mu
    rstd = lax.rsqrt(var + LN_EPS)

    t_idx = lax.broadcasted_iota(jnp.int32, (GMLP_CHUNK, GMLP_CHUNK), 0)
    s_idx = lax.broadcasted_iota(jnp.int32, (GMLP_CHUNK, GMLP_CHUNK), 1)
    causal = s_idx <= t_idx

    for g in range(GMLP_GROUPS):
        lo = g * gw
        vn = ((v_scr[:, lo:lo + gw] - mu) * rstd * lng_ref[:, lo:lo + gw] + lnb_ref[:, lo:lo + gw]).astype(BF16)
        w = jnp.where(causal, ws_ref[g], 0.0).astype(BF16)
        z = jnp.dot(xn, win_ref[:, lo:lo + gw], preferred_element_type=F32) + bin_ref[:, lo:lo + gw]
        u = _gelu(z)
        for c in range(tm // GMLP_CHUNK):
            r = c * GMLP_CHUNK
            s = jnp.dot(w, vn[r:r + GMLP_CHUNK, :], preferred_element_type=F32) + bs_ref[:, lo:lo + gw]
            gated_scr[r:r + GMLP_CHUNK, lo:lo + gw] = (u[r:r + GMLP_CHUNK, :] * s).astype(BF16)

    y = jnp.dot(gated_scr[...], wout_ref[...], preferred_element_type=F32)
    y_ref[...] = x_ref[...] + y + bout_ref[...]


def _gmlp(x, layer, gains, gl, w_in, b_in, ln_g, ln_b, w_s, bs_full, w_out, b_out):
    t, d = x.shape
    half = w_out.shape[1]
    tm = TOKEN_TILE
    assert tm % GMLP_CHUNK == 0
    return pl.pallas_call(
        functools.partial(_gmlp_kernel, half=half),
        out_shape=jax.ShapeDtypeStruct((t, d), F32),
        grid=(t // tm,),
        in_specs=[
            pl.BlockSpec((tm, d), lambda i: (i, 0)),
            _layer_spec(layer, (1, d)),
            _layer_spec(gl, (d, 2 * half)),
            _layer_spec(gl, (1, 2 * half)),
            _layer_spec(gl, (1, half)),
            _layer_spec(gl, (1, half)),
            _layer_spec(gl, (GMLP_GROUPS, GMLP_CHUNK, GMLP_CHUNK)),
            _layer_spec(gl, (GMLP_CHUNK, half)),
            _layer_spec(gl, (half, d)),
            _layer_spec(gl, (1, d)),
        ],
        out_specs=pl.BlockSpec((tm, d), lambda i: (i, 0)),
        scratch_shapes=[pltpu.VMEM((tm, half), F32), pltpu.VMEM((tm, half), BF16)],
        compiler_params=_params(1),
        name="gmlp",
    )(x, gains, w_in, b_in, ln_g, ln_b, w_s, bs_full, w_out, b_out)


def _lambda_init(layer_idx):
    return 0.8 - 0.6 * math.exp(-0.3 * layer_idx)


def _rows(v):
    return v.reshape(v.shape[0], 1, v.shape[1]).astype(F32)


def kernel(x, ffn1_norm, ffn1_w_gate_up, ffn1_w_down, mix_norm, ffn2_norm, ffn2_w_gate_up, ffn2_w_down, attn_w_in, attn_w_out, attn_q_norm, attn_k_norm, attn_lambda_q1, attn_lambda_k1, attn_lambda_q2, attn_lambda_k2, attn_subln, gmlp_w_in, gmlp_b_in, gmlp_ln_g, gmlp_ln_b, gmlp_w_s, gmlp_b_s, gmlp_w_out, gmlp_b_out):
    b, s, d = x.shape
    depth = ffn1_norm.shape[0]
    h = DIFF_HEADS
    hd = d // h // 2
    nq = s // ATTN_TILE
    xf = x.reshape(b * s, d)

    ffn_gains = (_rows(ffn1_norm), _rows(ffn2_norm))
    ffn_f32 = ((ffn1_w_gate_up, ffn1_w_down), (ffn2_w_gate_up, ffn2_w_down))
    mix_gains = _rows(mix_norm)
    a_w_in = attn_w_in.astype(BF16)
    a_w_in_t = jnp.swapaxes(a_w_in, 1, 2)
    a_w_out = attn_w_out.astype(BF16)
    feat_gain = attn_q_norm.astype(F32) * attn_k_norm.astype(F32) * (hd ** -0.5 * LOG2_E)
    kgain = _rows(jnp.tile(feat_gain, (1, 2 * h)))
    score_bound = 1.01 * hd * jnp.max(jnp.abs(feat_gain), axis=1)
    stable = (score_bound > SCORE_BOUND_LOG2).astype(jnp.int32)
    lam_vecs = [_rows(v) for v in (attn_lambda_q1, attn_lambda_k1, attn_lambda_q2, attn_lambda_k2)]
    subln_col = attn_subln.reshape(attn_subln.shape[0], -1, 1).astype(F32)
    half = gmlp_w_out.shape[1]
    g_params = (gmlp_w_in.astype(BF16), _rows(gmlp_b_in), _rows(gmlp_ln_g), _rows(gmlp_ln_b), gmlp_w_s.astype(F32),
                jnp.repeat(jnp.swapaxes(gmlp_b_s, 1, 2), half // GMLP_GROUPS, axis=2).astype(F32),
                gmlp_w_out.astype(BF16), _rows(gmlp_b_out))

    w_bf16 = [ffn1_w_gate_up[0:1].astype(BF16), ffn1_w_down[0:1].astype(BF16)]

    def ffn(xf, which, i, proj=None):
        nxt = (1, i) if which == 0 else (0, i + 1)
        cast = tuple((w, nxt[1]) for w in ffn_f32[nxt[0]]) if nxt[1] < depth else ()
        xf, *w_next = _ffn(xf, i, ffn_gains[which], *w_bf16, proj=proj, cast=cast, w_layer=0)
        w_bf16[:] = w_next
        return xf

    proj = None
    for i in range(depth):
        xf = ffn(xf, 0, i)
        j = i // 2
        if i % 2 == 0:
            k, qt, vt = _attn_in(xf, i, mix_gains, j, a_w_in, a_w_in_t, kgain, hd)
            o = _flash(stable[j:j + 1], j, qt.reshape(b, nq, d, ATTN_TILE), k.reshape(b, s, d),
                       vt.reshape(b, nq, d, ATTN_TILE), *lam_vecs, subln_col, _lambda_init(i))
            proj = (o.reshape(b * s, d), a_w_out, j)
        else:
            xf = _gmlp(xf, i, mix_gains, j, *g_params)
            proj = None
        xf = ffn(xf, 1, i, proj=proj)
    return xf.reshape(b, s, d)
```

```python
import functools
import math

import jax
import jax.numpy as jnp
from jax import lax
from jax.experimental import pallas as pl
from jax.experimental.pallas import tpu as pltpu

F32 = jnp.float32
BF16 = jnp.bfloat16

RMS_EPS = 1e-6
LN_EPS = 1e-5
DIFF_HEADS = 8
GMLP_GROUPS = 8
GMLP_CHUNK = 128

MXU_TILE = 256
VMEM_LIMIT_BYTES = 56 * 1024 * 1024
NEG_BIG = -1e30
LOG2_E = math.log2(math.e)
SCORE_BOUND_LOG2 = 60.0

TOKEN_TILE = 512
FFN_TILE = 1024
ATTN_TILE = 512
Q_TILES = 4
KV_BLOCKS_PER_STEP = 4


def _layer_spec(layer, shape, block_idx=None):
    block_idx = tuple(block_idx) if block_idx is not None else (0,) * len(shape)
    return pl.BlockSpec((None,) + tuple(shape), lambda *_: (layer,) + block_idx, pipeline_mode=pl.Buffered(1))


def _params(n_axes):
    return pltpu.CompilerParams(
        dimension_semantics=("arbitrary",) * n_axes,
        vmem_limit_bytes=VMEM_LIMIT_BYTES,
    )


def _rms_rows(x, gain):
    ms = jnp.mean(x * x, axis=-1, keepdims=True)
    return x * lax.rsqrt(ms + RMS_EPS) * gain


def _ffn_kernel(*refs, d_ff, with_proj, n_cast):
    refs = list(refs)
    x_ref = refs.pop(0)
    a_ref, wp_ref = (refs.pop(0), refs.pop(0)) if with_proj else (None, None)
    g_ref, wgu_ref, wd_ref = refs.pop(0), refs.pop(0), refs.pop(0)
    cast_src = [refs.pop(0) for _ in range(n_cast)]
    o_ref = refs.pop(0)
    cast_dst = [refs.pop(0) for _ in range(n_cast)]
    (h_ref,) = refs
    for src, dst in zip(cast_src, cast_dst):
        dst[...] = src[...].astype(BF16)
    if with_proj:
        o_ref[...] = x_ref[...] + jnp.dot(a_ref[...], wp_ref[...], preferred_element_type=F32)
    else:
        o_ref[...] = x_ref[...]
    xn = _rms_rows(o_ref[...], g_ref[...]).astype(BF16)
    for j in range(d_ff // MXU_TILE):
        lo = j * MXU_TILE
        gate = jnp.dot(xn, wgu_ref[:, lo:lo + MXU_TILE], preferred_element_type=F32)
        up = jnp.dot(xn, wgu_ref[:, d_ff + lo:d_ff + lo + MXU_TILE], preferred_element_type=F32)
        h_ref[:, lo:lo + MXU_TILE] = (gate * jax.nn.sigmoid(gate) * up).astype(BF16)
    y = jnp.dot(h_ref[...], wd_ref[...], preferred_element_type=F32)
    o_ref[...] = o_ref[...] + 0.5 * y


def _ffn(x, layer, gains, w_gate_up, w_down, proj=None, cast=(), w_layer=0):
    t, d = x.shape
    d_ff = w_down.shape[1]
    assert d_ff % MXU_TILE == 0 and t % FFN_TILE == 0
    tm = FFN_TILE
    steps = t // tm
    row_spec = pl.BlockSpec((tm, d), lambda i: (i, 0))
    operands, in_specs = [x], [row_spec]
    if proj is not None:
        a, w_proj, proj_layer = proj
        operands += [a, w_proj]
        in_specs += [pl.BlockSpec((tm, a.shape[1]), lambda i: (i, 0)), _layer_spec(proj_layer, w_proj.shape[1:])]
    operands += [gains, w_gate_up, w_down]
    in_specs += [_layer_spec(layer, (1, d)), _layer_spec(w_layer, (d, 2 * d_ff)), _layer_spec(w_layer, (d_ff, d))]
    out_shape, out_specs = [jax.ShapeDtypeStruct((t, d), F32)], [row_spec]
    for stack, src_layer in cast:
        n_layers, rows, cols = stack.shape
        assert rows % steps == 0
        chunk = rows // steps
        operands.append(stack.reshape(n_layers, steps, chunk, cols))
        in_specs.append(pl.BlockSpec((None, None, chunk, cols), lambda i, src_layer=src_layer: (src_layer, i, 0, 0)))
        out_shape.append(jax.ShapeDtypeStruct((steps, chunk, cols), BF16))
        out_specs.append(pl.BlockSpec((None, chunk, cols), lambda i: (i, 0, 0)))
    outs = pl.pallas_call(
        functools.partial(_ffn_kernel, d_ff=d_ff, with_proj=proj is not None, n_cast=len(cast)),
        out_shape=out_shape,
        grid=(steps,),
        in_specs=in_specs,
        out_specs=out_specs,
        scratch_shapes=[pltpu.VMEM((tm, d_ff), BF16)],
        compiler_params=_params(1),
        name="ffn",
    )(*operands)
    return [outs[0]] + [o.reshape(1, stack.shape[1], stack.shape[2]) for o, (stack, _) in zip(outs[1:], cast)]


def _group_ones(n, group):
    r = lax.broadcasted_iota(jnp.int32, (n, n), 0) // group
    c = lax.broadcasted_iota(jnp.int32, (n, n), 1) // group
    return (r == c).astype(BF16)


def _attn_in_kernel(x_ref, g_ref, wk_ref, wqt_ref, wvt_ref, kgain_ref, k_ref, qt_ref, vt_ref, *, hd):
    d = x_ref.shape[1]
    xn = _rms_rows(x_ref[...], g_ref[...]).astype(BF16)
    ones = _group_ones(MXU_TILE, hd)
    nt_dims = (((1,), (1,)), ((), ()))

    k = jnp.dot(xn, wk_ref[...], preferred_element_type=F32)
    for j in range(d // MXU_TILE):
        lo = j * MXU_TILE
        kj = k[:, lo:lo + MXU_TILE]
        ss = jnp.dot((kj * kj).astype(BF16), ones, preferred_element_type=F32)
        kn = kj * lax.rsqrt(ss * (1.0 / hd) + RMS_EPS) * kgain_ref[:, lo:lo + MXU_TILE]
        k_ref[:, lo:lo + MXU_TILE] = kn.astype(BF16)

    qt = lax.dot_general(wqt_ref[...], xn, nt_dims, preferred_element_type=F32)
    for j in range(d // MXU_TILE):
        lo = j * MXU_TILE
        qj = qt[lo:lo + MXU_TILE, :]
        ss = jnp.dot(ones, (qj * qj).astype(BF16), preferred_element_type=F32)
        qt_ref[0, lo:lo + MXU_TILE, :] = (qj * lax.rsqrt(ss * (1.0 / hd) + RMS_EPS)).astype(BF16)

    vt = lax.dot_general(wvt_ref[...], xn, nt_dims, preferred_element_type=F32)
    vt_ref[0] = vt.astype(BF16)


def _attn_in(x, layer, gains, attn_layer, w_in, w_in_t, kgain, hd):
    t, d = x.shape
    tm = ATTN_TILE
    nt = t // tm
    return pl.pallas_call(
        functools.partial(_attn_in_kernel, hd=hd),
        out_shape=(
            jax.ShapeDtypeStruct((t, d), BF16),
            jax.ShapeDtypeStruct((nt, d, tm), BF16),
            jax.ShapeDtypeStruct((nt, d, tm), BF16),
        ),
        grid=(nt,),
        in_specs=[
            pl.BlockSpec((tm, d), lambda i: (i, 0)),
            _layer_spec(layer, (1, d)),
            _layer_spec(attn_layer, (d, d), (0, 1)),
            _layer_spec(attn_layer, (d, d), (0, 0)),
            _layer_spec(attn_layer, (d, d), (2, 0)),
            _layer_spec(attn_layer, (1, d)),
        ],
        out_specs=(
            pl.BlockSpec((tm, d), lambda i: (i, 0)),
            pl.BlockSpec((1, d, tm), lambda i: (i, 0, 0)),
            pl.BlockSpec((1, d, tm), lambda i: (i, 0, 0)),
        ),
        compiler_params=_params(1),
        name="attn_in",
    )(x, gains, w_in, w_in_t, w_in_t, kgain)


def _flash_kernel(stable_ref, qt_ref, k_ref, vt_ref, lq1_ref, lk1_ref, lq2_ref, lk2_ref, sub_ref, o_ref,
                  q_scr, m_scr, l_scr, acc_scr, *, hd, lam_init, nq, n_tiles):
    g = pl.program_id(0)
    i = lax.rem(g, nq)
    parity = lax.rem(g, 2)
    tk = vt_ref.shape[3]
    tq = Q_TILES * tk

    def finish(sl):
        lam = (jnp.exp(jnp.sum(lq1_ref[...] * lk1_ref[...], axis=-1, keepdims=True))
               - jnp.exp(jnp.sum(lq2_ref[...] * lk2_ref[...], axis=-1, keepdims=True))
               + lam_init)
        o = acc_scr[sl, 0] * (1.0 / l_scr[sl, 0]) - acc_scr[sl, 1] * (lam / l_scr[sl, 1])
        ms = jnp.mean(o * o, axis=0, keepdims=True)
        on = o * lax.rsqrt(ms + RMS_EPS) * (sub_ref[...] * (1.0 - lam_init))
        o_ref[0] = on.T.astype(BF16)

    def kv_block(slot, j, nblk, q_lo, diagonal, stable):
        start = pl.multiple_of(j * tk, tk)
        kj = k_ref[0, pl.ds(start, nblk * tk), :]
        vj = jnp.concatenate([vt_ref[0, j + n] for n in range(nblk)], axis=1)
        for c in range(2):
            s = jnp.dot(kj, q_scr[c, :, q_lo:], preferred_element_type=F32)
            if diagonal:
                kpos = lax.broadcasted_iota(jnp.int32, s.shape, 0) - (nblk - 1) * tk
                qpos = lax.broadcasted_iota(jnp.int32, s.shape, 1)
                s = jnp.where(kpos <= qpos, s, NEG_BIG)
            if stable:
                m_old = m_scr[c, :, q_lo:]
                m_new = jnp.maximum(m_old, jnp.max(s, axis=0, keepdims=True))
                alpha = jnp.exp2(m_old - m_new)
                p = jnp.exp2(s - m_new)
                l_scr[slot, c, :, q_lo:] = alpha * l_scr[slot, c, :, q_lo:] + jnp.sum(p, axis=0, keepdims=True)
                acc_scr[slot, c, :, q_lo:] = (alpha * acc_scr[slot, c, :, q_lo:]
                                              + jnp.dot(vj, p.astype(BF16), preferred_element_type=F32))
                m_scr[c, :, q_lo:] = m_new
            else:
                p = jnp.exp2(s)
                l_scr[slot, c, :, q_lo:] = l_scr[slot, c, :, q_lo:] + jnp.sum(p, axis=0, keepdims=True)
                acc_scr[slot, c, :, q_lo:] = (acc_scr[slot, c, :, q_lo:]
                                              + jnp.dot(vj, p.astype(BF16), preferred_element_type=F32))

    def run(slot, stable, chunk):
        qt = jnp.concatenate([qt_ref[0, n] for n in range(Q_TILES)], axis=1)
        row = lax.broadcasted_iota(jnp.int32, qt.shape, 0)
        zero = jnp.zeros_like(qt)
        q_scr[0] = jnp.where(row < hd, qt, zero)
        q_scr[1] = jnp.where(row >= hd, qt, zero)
        m_scr[...] = jnp.full(m_scr.shape, NEG_BIG, F32)
        l_scr[slot] = jnp.zeros(l_scr.shape[1:], F32)
        acc_scr[slot] = jnp.zeros(acc_scr.shape[1:], F32)

        n_below = Q_TILES * i
        n_trips = n_below // chunk

        def body(jj, carry):
            kv_block(slot, jj * chunk, chunk, 0, diagonal=False, stable=stable)
            return carry
        lax.fori_loop(0, n_trips, body, 0)

        def tail(rem):
            finish(1 - slot)
            kv_block(slot, n_trips * chunk, rem + 1, 0, diagonal=True, stable=stable)
            for n in range(1, Q_TILES):
                kv_block(slot, n_below + n, 1, n * tk, diagonal=True, stable=stable)

        rems = range(0, chunk, Q_TILES)
        if len(rems) == 1:
            tail(0)
        else:
            for rem in rems:
                pl.when(n_below - n_trips * chunk == rem)(functools.partial(tail, rem))

    @pl.when(g == 0)
    def _():
        l_scr[1] = jnp.ones(l_scr.shape[1:], F32)
        acc_scr[1] = jnp.zeros(acc_scr.shape[1:], F32)

    fast = jnp.logical_and(g < n_tiles, stable_ref[0] == 0)
    for static_slot in range(2):
        pl.when(jnp.logical_and(fast, parity == static_slot))(
            functools.partial(run, static_slot, stable=False, chunk=KV_BLOCKS_PER_STEP))

    @pl.when(jnp.logical_and(g < n_tiles, stable_ref[0] != 0))
    def _():
        run(parity, stable=True, chunk=1)

    @pl.when(g == n_tiles)
    def _():
        finish(1 - n_tiles % 2)


def _flash(stable, attn_layer, qt, k, vt, lq1, lk1, lq2, lk2, subln_col, lam_init):
    b, nk, d, tk = vt.shape
    _, s, _ = k.shape
    assert qt.shape == vt.shape and nk % Q_TILES == 0 and KV_BLOCKS_PER_STEP % Q_TILES == 0
    nq = nk // Q_TILES
    tq = Q_TILES * tk
    h = DIFF_HEADS
    hw = d // h
    hd = hw // 2
    n_tiles = b * h * nq

    def tile(t):
        return t // (h * nq), lax.rem(t, nq), lax.rem(t // nq, h)

    def cur(g):
        return tile(jnp.minimum(g, n_tiles - 1))

    def qt_map(g, st):
        bb, ii, hh = cur(g)
        return bb, ii, hh, 0

    def k_map(g, st):
        bb, _, hh = cur(g)
        return bb, 0, hh

    def vt_map(g, st):
        bb, _, hh = cur(g)
        return bb, 0, hh, 0

    def out_map(g, st):
        bb, ii, hh = tile(jnp.maximum(g - 1, 0))
        return bb, ii, hh

    vec = pl.BlockSpec((None, 1, hd), lambda g, st: (attn_layer, 0, 0))
    grid_spec = pltpu.PrefetchScalarGridSpec(
        num_scalar_prefetch=1,
        grid=(n_tiles + 1,),
        in_specs=[
            pl.BlockSpec((1, Q_TILES, hw, tk), qt_map),
            pl.BlockSpec((1, s, hw), k_map),
            pl.BlockSpec((1, nk, hw, tk), vt_map),
            vec, vec, vec, vec,
            pl.BlockSpec((None, hw, 1), lambda g, st: (attn_layer, 0, 0)),
        ],
        out_specs=pl.BlockSpec((1, tq, hw), out_map),
        scratch_shapes=[
            pltpu.VMEM((2, hw, tq), BF16),
            pltpu.VMEM((2, 1, tq), F32),
            pltpu.VMEM((2, 2, 1, tq), F32),
            pltpu.VMEM((2, 2, hw, tq), F32),
        ],
    )
    return pl.pallas_call(
        functools.partial(_flash_kernel, hd=hd, lam_init=lam_init, nq=nq, n_tiles=n_tiles),
        out_shape=jax.ShapeDtypeStruct((b, s, d), BF16),
        grid_spec=grid_spec,
        compiler_params=_params(1),
        name="flash",
    )(stable, qt, k, vt, lq1, lk1, lq2, lk2, subln_col)


def _gelu(z):
    return 0.5 * z * (1.0 + lax.erf(z * math.sqrt(0.5)))


def _gmlp_kernel(x_ref, g_ref, win_ref, bin_ref, lng_ref, lnb_ref, ws_ref, bs_ref, wout_ref, bout_ref,
                 y_ref, v_scr, gated_scr, *, half):
    tm = x_ref.shape[0]
    gw = half // GMLP_GROUPS
    xn = _rms_rows(x_ref[...], g_ref[...]).astype(BF16)

    rsum = jnp.zeros((tm, 1), F32)
    rsq = jnp.zeros((tm, 1), F32)
    for g in range(GMLP_GROUPS):
        lo = half + g * gw
        z = jnp.dot(xn, win_ref[:, lo:lo + gw], preferred_element_type=F32) + bin_ref[:, lo:lo + gw]
        v = _gelu(z)
        v_scr[:, g * gw:(g + 1) * gw] = v
        rsum = rsum + jnp.sum(v, axis=-1, keepdims=True)
        rsq = rsq + jnp.sum(v * v, axis=-1, keepdims=True)
    mu = rsum * (1.0 / half)
    var = rsq * (1.0 / half) - mu * mu
    rstd = lax.rsqrt(var + LN_EPS)

    t_idx = lax.broadcasted_iota(jnp.int32, (GMLP_CHUNK, GMLP_CHUNK), 0)
    s_idx = lax.broadcasted_iota(jnp.int32, (GMLP_CHUNK, GMLP_CHUNK), 1)
    causal = s_idx <= t_idx

    for g in range(GMLP_GROUPS):
        lo = g * gw
        vn = ((v_scr[:, lo:lo + gw] - mu) * rstd * lng_ref[:, lo:lo + gw] + lnb_ref[:, lo:lo + gw]).astype(BF16)
        w = jnp.where(causal, ws_ref[g], 0.0).astype(BF16)
        z = jnp.dot(xn, win_ref[:, lo:lo + gw], preferred_element_type=F32) + bin_ref[:, lo:lo + gw]
        u = _gelu(z)
        for c in range(tm // GMLP_CHUNK):
            r = c * GMLP_CHUNK
            s = jnp.dot(w, vn[r:r + GMLP_CHUNK, :], preferred_element_type=F32) + bs_ref[:, lo:lo + gw]
            gated_scr[r:r + GMLP_CHUNK, lo:lo + gw] = (u[r:r + GMLP_CHUNK, :] * s).astype(BF16)

    y = jnp.dot(gated_scr[...], wout_ref[...], preferred_element_type=F32)
    y_ref[...] = x_ref[...] + y + bout_ref[...]


def _gmlp(x, layer, gains, gl, w_in, b_in, ln_g, ln_b, w_s, bs_full, w_out, b_out):
    t, d = x.shape
    half = w_out.shape[1]
    tm = TOKEN_TILE
    assert tm % GMLP_CHUNK == 0
    return pl.pallas_call(
        functools.partial(_gmlp_kernel, half=half),
        out_shape=jax.ShapeDtypeStruct((t, d), F32),
        grid=(t // tm,),
        in_specs=[
            pl.BlockSpec((tm, d), lambda i: (i, 0)),
            _layer_spec(layer, (1, d)),
            _layer_spec(gl, (d, 2 * half)),
            _layer_spec(gl, (1, 2 * half)),
            _layer_spec(gl, (1, half)),
            _layer_spec(gl, (1, half)),
            _layer_spec(gl, (GMLP_GROUPS, GMLP_CHUNK, GMLP_CHUNK)),
            _layer_spec(gl, (GMLP_CHUNK, half)),
            _layer_spec(gl, (half, d)),
            _layer_spec(gl, (1, d)),
        ],
        out_specs=pl.BlockSpec((tm, d), lambda i: (i, 0)),
        scratch_shapes=[pltpu.VMEM((tm, half), F32), pltpu.VMEM((tm, half), BF16)],
        compiler_params=_params(1),
        name="gmlp",
    )(x, gains, w_in, b_in, ln_g, ln_b, w_s, bs_full, w_out, b_out)


def _lambda_init(layer_idx):
    return 0.8 - 0.6 * math.exp(-0.3 * layer_idx)


def _rows(v):
    return v.reshape(v.shape[0], 1, v.shape[1]).astype(F32)


def kernel(x, ffn1_norm, ffn1_w_gate_up, ffn1_w_down, mix_norm, ffn2_norm, ffn2_w_gate_up, ffn2_w_down, attn_w_in, attn_w_out, attn_q_norm, attn_k_norm, attn_lambda_q1, attn_lambda_k1, attn_lambda_q2, attn_lambda_k2, attn_subln, gmlp_w_in, gmlp_b_in, gmlp_ln_g, gmlp_ln_b, gmlp_w_s, gmlp_b_s, gmlp_w_out, gmlp_b_out):
    b, s, d = x.shape
    depth = ffn1_norm.shape[0]
    h = DIFF_HEADS
    hd = d // h // 2
    nq = s // ATTN_TILE
    xf = x.reshape(b * s, d)

    ffn_gains = (_rows(ffn1_norm), _rows(ffn2_norm))
    ffn_f32 = ((ffn1_w_gate_up, ffn1_w_down), (ffn2_w_gate_up, ffn2_w_down))
    mix_gains = _rows(mix_norm)
    a_w_in = attn_w_in.astype(BF16)
    a_w_in_t = jnp.swapaxes(a_w_in, 1, 2)
    a_w_out = attn_w_out.astype(BF16)
    feat_gain = attn_q_norm.astype(F32) * attn_k_norm.astype(F32) * (hd ** -0.5 * LOG2_E)
    kgain = _rows(jnp.tile(feat_gain, (1, 2 * h)))
    score_bound = 1.01 * hd * jnp.max(jnp.abs(feat_gain), axis=1)
    stable = (score_bound > SCORE_BOUND_LOG2).astype(jnp.int32)
    lam_vecs = [_rows(v) for v in (attn_lambda_q1, attn_lambda_k1, attn_lambda_q2, attn_lambda_k2)]
    subln_col = attn_subln.reshape(attn_subln.shape[0], -1, 1).astype(F32)
    half = gmlp_w_out.shape[1]
    g_params = (gmlp_w_in.astype(BF16), _rows(gmlp_b_in), _rows(gmlp_ln_g), _rows(gmlp_ln_b), gmlp_w_s.astype(F32),
                jnp.repeat(jnp.swapaxes(gmlp_b_s, 1, 2), half // GMLP_GROUPS, axis=2).astype(F32),
                gmlp_w_out.astype(BF16), _rows(gmlp_b_out))

    w_bf16 = [ffn1_w_gate_up[0:1].astype(BF16), ffn1_w_down[0:1].astype(BF16)]

    def ffn(xf, which, i, proj=None):
        nxt = (1, i) if which == 0 else (0, i + 1)
        cast = tuple((w, nxt[1]) for w in ffn_f32[nxt[0]]) if nxt[1] < depth else ()
        xf, *w_next = _ffn(xf, i, ffn_gains[which], *w_bf16, proj=proj, cast=cast, w_layer=0)
        w_bf16[:] = w_next
        return xf

    proj = None
    for i in range(depth):
        xf = ffn(xf, 0, i)
        j = i // 2
        if i % 2 == 0:
            k, qt, vt = _attn_in(xf, i, mix_gains, j, a_w_in, a_w_in_t, kgain, hd)
            o = _flash(stable[j:j + 1], j, qt.reshape(b, nq, d, ATTN_TILE), k.reshape(b, s, d),
                       vt.reshape(b, nq, d, ATTN_TILE), *lam_vecs, subln_col, _lambda_init(i))
            proj = (o.reshape(b * s, d), a_w_out, j)
        else:
            xf = _gmlp(xf, i, mix_gains, j, *g_params)
            proj = None
        xf = ffn(xf, 1, i, proj=proj)
    return xf.reshape(b, s, d)
```

```python
import functools
import math

import jax
import jax.numpy as jnp
from jax import lax
from jax.experimental import pallas as pl
from jax.experimental.pallas import tpu as pltpu

F32 = jnp.float32
BF16 = jnp.bfloat16

RMS_EPS = 1e-6
LN_EPS = 1e-5
DIFF_HEADS = 8
GMLP_GROUPS = 8
GMLP_CHUNK = 128

MXU_TILE = 256
VMEM_LIMIT_BYTES = 56 * 1024 * 1024
NEG_BIG = -1e30
LOG2_E = math.log2(math.e)
SCORE_BOUND_LOG2 = 60.0

GMLP_TILE = 1024
FFN_TILE = 1024
ATTN_TILE = 512
Q_TILES = 4
KV_BLOCKS_PER_STEP = 4


def _layer_spec(layer, shape, block_idx=None):
    block_idx = tuple(block_idx) if block_idx is not None else (0,) * len(shape)
    return pl.BlockSpec((None,) + tuple(shape), lambda *_: (layer,) + block_idx, pipeline_mode=pl.Buffered(1))


def _params(n_axes):
    return pltpu.CompilerParams(
        dimension_semantics=("arbitrary",) * n_axes,
        vmem_limit_bytes=VMEM_LIMIT_BYTES,
    )


def _rms_rows(x, gain):
    ms = jnp.mean(x * x, axis=-1, keepdims=True)
    return x * lax.rsqrt(ms + RMS_EPS) * gain


def _ffn_kernel(*refs, d_ff, with_proj, n_cast):
    refs = list(refs)
    x_ref = refs.pop(0)
    a_ref, wp_ref = (refs.pop(0), refs.pop(0)) if with_proj else (None, None)
    g_ref, wgu_ref, wd_ref = refs.pop(0), refs.pop(0), refs.pop(0)
    cast_src = [refs.pop(0) for _ in range(n_cast)]
    o_ref = refs.pop(0)
    cast_dst = [refs.pop(0) for _ in range(n_cast)]
    (h_ref,) = refs
    for src, dst in zip(cast_src, cast_dst):
        dst[...] = src[...].astype(BF16)
    if with_proj:
        o_ref[...] = x_ref[...] + jnp.dot(a_ref[...], wp_ref[...], preferred_element_type=F32)
    else:
        o_ref[...] = x_ref[...]
    xn = _rms_rows(o_ref[...], g_ref[...]).astype(BF16)
    for j in range(d_ff // MXU_TILE):
        lo = j * MXU_TILE
        gate = jnp.dot(xn, wgu_ref[:, lo:lo + MXU_TILE], preferred_element_type=F32)
        up = jnp.dot(xn, wgu_ref[:, d_ff + lo:d_ff + lo + MXU_TILE], preferred_element_type=F32)
        h_ref[:, lo:lo + MXU_TILE] = (gate * jax.nn.sigmoid(gate) * up).astype(BF16)
    y = jnp.dot(h_ref[...], wd_ref[...], preferred_element_type=F32)
    o_ref[...] = o_ref[...] + 0.5 * y


def _ffn(x, layer, gains, w_gate_up, w_down, proj=None, cast=(), w_layer=0):
    t, d = x.shape
    d_ff = w_down.shape[1]
    assert d_ff % MXU_TILE == 0 and t % FFN_TILE == 0
    tm = FFN_TILE
    steps = t // tm
    row_spec = pl.BlockSpec((tm, d), lambda i: (i, 0))
    operands, in_specs = [x], [row_spec]
    if proj is not None:
        a, w_proj, proj_layer = proj
        operands += [a, w_proj]
        in_specs += [pl.BlockSpec((tm, a.shape[1]), lambda i: (i, 0)), _layer_spec(proj_layer, w_proj.shape[1:])]
    operands += [gains, w_gate_up, w_down]
    in_specs += [_layer_spec(layer, (1, d)), _layer_spec(w_layer, (d, 2 * d_ff)), _layer_spec(w_layer, (d_ff, d))]
    out_shape, out_specs = [jax.ShapeDtypeStruct((t, d), F32)], [row_spec]
    for stack, src_layer in cast:
        n_layers, rows, cols = stack.shape
        assert rows % steps == 0
        chunk = rows // steps
        operands.append(stack.reshape(n_layers, steps, chunk, cols))
        in_specs.append(pl.BlockSpec((None, None, chunk, cols), lambda i, src_layer=src_layer: (src_layer, i, 0, 0)))
        out_shape.append(jax.ShapeDtypeStruct((steps, chunk, cols), BF16))
        out_specs.append(pl.BlockSpec((None, chunk, cols), lambda i: (i, 0, 0)))
    outs = pl.pallas_call(
        functools.partial(_ffn_kernel, d_ff=d_ff, with_proj=proj is not None, n_cast=len(cast)),
        out_shape=out_shape,
        grid=(steps,),
        in_specs=in_specs,
        out_specs=out_specs,
        scratch_shapes=[pltpu.VMEM((tm, d_ff), BF16)],
        compiler_params=_params(1),
        name="ffn",
    )(*operands)
    return [outs[0]] + [o.reshape(1, stack.shape[1], stack.shape[2]) for o, (stack, _) in zip(outs[1:], cast)]


def _group_ones(n, group):
    r = lax.broadcasted_iota(jnp.int32, (n, n), 0) // group
    c = lax.broadcasted_iota(jnp.int32, (n, n), 1) // group
    return (r == c).astype(BF16)


def _attn_in_kernel(x_ref, g_ref, wk_ref, wqt_ref, wvt_ref, kgain_ref, k_ref, qt_ref, vt_ref, *, hd):
    d = x_ref.shape[1]
    xn = _rms_rows(x_ref[...], g_ref[...]).astype(BF16)
    ones = _group_ones(MXU_TILE, hd)
    nt_dims = (((1,), (1,)), ((), ()))

    k = jnp.dot(xn, wk_ref[...], preferred_element_type=F32)
    for j in range(d // MXU_TILE):
        lo = j * MXU_TILE
        kj = k[:, lo:lo + MXU_TILE]
        ss = jnp.dot((kj * kj).astype(BF16), ones, preferred_element_type=F32)
        kn = kj * lax.rsqrt(ss * (1.0 / hd) + RMS_EPS) * kgain_ref[:, lo:lo + MXU_TILE]
        k_ref[:, lo:lo + MXU_TILE] = kn.astype(BF16)

    qt = lax.dot_general(wqt_ref[...], xn, nt_dims, preferred_element_type=F32)
    for j in range(d // MXU_TILE):
        lo = j * MXU_TILE
        qj = qt[lo:lo + MXU_TILE, :]
        ss = jnp.dot(ones, (qj * qj).astype(BF16), preferred_element_type=F32)
        qt_ref[0, lo:lo + MXU_TILE, :] = (qj * lax.rsqrt(ss * (1.0 / hd) + RMS_EPS)).astype(BF16)

    vt = lax.dot_general(wvt_ref[...], xn, nt_dims, preferred_element_type=F32)
    vt_ref[0] = vt.astype(BF16)


def _attn_in(x, layer, gains, attn_layer, w_k, w_q_t, w_v_t, kgain, hd):
    t, d = x.shape
    tm = ATTN_TILE
    nt = t // tm
    return pl.pallas_call(
        functools.partial(_attn_in_kernel, hd=hd),
        out_shape=(
            jax.ShapeDtypeStruct((t, d), BF16),
            jax.ShapeDtypeStruct((nt, d, tm), BF16),
            jax.ShapeDtypeStruct((nt, d, tm), BF16),
        ),
        grid=(nt,),
        in_specs=[
            pl.BlockSpec((tm, d), lambda i: (i, 0)),
            _layer_spec(layer, (1, d)),
            _layer_spec(attn_layer, (d, d)),
            _layer_spec(attn_layer, (d, d)),
            _layer_spec(attn_layer, (d, d)),
            _layer_spec(attn_layer, (1, d)),
        ],
        out_specs=(
            pl.BlockSpec((tm, d), lambda i: (i, 0)),
            pl.BlockSpec((1, d, tm), lambda i: (i, 0, 0)),
            pl.BlockSpec((1, d, tm), lambda i: (i, 0, 0)),
        ),
        compiler_params=_params(1),
        name="attn_in",
    )(x, gains, w_k, w_q_t, w_v_t, kgain)


def _flash_kernel(stable_ref, qt_ref, k_ref, vt_ref, lq1_ref, lk1_ref, lq2_ref, lk2_ref, sub_ref, o_ref,
                  q_scr, m_scr, l_scr, acc_scr, *, hd, lam_init):
    i = pl.program_id(2)
    tk = vt_ref.shape[3]
    tq = Q_TILES * tk

    qt = jnp.concatenate([qt_ref[0, n] for n in range(Q_TILES)], axis=1)
    row = lax.broadcasted_iota(jnp.int32, qt.shape, 0)
    zero = jnp.zeros_like(qt)
    q_scr[0] = jnp.where(row < hd, qt, zero)
    q_scr[1] = jnp.where(row >= hd, qt, zero)
    m_scr[...] = jnp.full(m_scr.shape, NEG_BIG, F32)
    l_scr[...] = jnp.zeros(l_scr.shape, F32)
    acc_scr[...] = jnp.zeros(acc_scr.shape, F32)

    def kv_block(j, nblk, q_lo, diagonal, stable):
        start = pl.multiple_of(j * tk, tk)
        kj = k_ref[0, pl.ds(start, nblk * tk), :]
        vj = jnp.concatenate([vt_ref[0, j + n] for n in range(nblk)], axis=1)
        for c in range(2):
            s = jnp.dot(kj, q_scr[c, :, q_lo:], preferred_element_type=F32)
            if diagonal:
                kpos = lax.broadcasted_iota(jnp.int32, s.shape, 0) - (nblk - 1) * tk
                qpos = lax.broadcasted_iota(jnp.int32, s.shape, 1)
                s = jnp.where(kpos <= qpos, s, NEG_BIG)
            if stable:
                m_old = m_scr[c, :, q_lo:]
                m_new = jnp.maximum(m_old, jnp.max(s, axis=0, keepdims=True))
                alpha = jnp.exp2(m_old - m_new)
                p = jnp.exp2(s - m_new)
                l_scr[c, :, q_lo:] = alpha * l_scr[c, :, q_lo:] + jnp.sum(p, axis=0, keepdims=True)
                acc_scr[c, :, q_lo:] = (alpha * acc_scr[c, :, q_lo:]
                                        + jnp.dot(vj, p.astype(BF16), preferred_element_type=F32))
                m_scr[c, :, q_lo:] = m_new
            else:
                p = jnp.exp2(s)
                l_scr[c, :, q_lo:] = l_scr[c, :, q_lo:] + jnp.sum(p, axis=0, keepdims=True)
                acc_scr[c, :, q_lo:] = acc_scr[c, :, q_lo:] + jnp.dot(vj, p.astype(BF16), preferred_element_type=F32)

    def run(stable, chunk):
        n_below = Q_TILES * i
        n_trips = n_below // chunk

        def body(jj, carry):
            kv_block(jj * chunk, chunk, 0, diagonal=False, stable=stable)
            return carry
        lax.fori_loop(0, n_trips, body, 0)

        for rem in range(0, chunk, Q_TILES):
            @pl.when(n_below - n_trips * chunk == rem)
            def _():
                kv_block(n_trips * chunk, rem + 1, 0, diagonal=True, stable=stable)
                for n in range(1, Q_TILES):
                    kv_block(n_below + n, 1, n * tk, diagonal=True, stable=stable)

    @pl.when(stable_ref[0] == 0)
    def _():
        run(stable=False, chunk=KV_BLOCKS_PER_STEP)

    @pl.when(stable_ref[0] != 0)
    def _():
        run(stable=True, chunk=1)

    lam = (jnp.exp(jnp.sum(lq1_ref[...] * lk1_ref[...], axis=-1, keepdims=True))
           - jnp.exp(jnp.sum(lq2_ref[...] * lk2_ref[...], axis=-1, keepdims=True))
           + lam_init)
    o = acc_scr[0] * (1.0 / l_scr[0]) - acc_scr[1] * (lam / l_scr[1])
    ms = jnp.mean(o * o, axis=0, keepdims=True)
    on = o * lax.rsqrt(ms + RMS_EPS) * (sub_ref[...] * (1.0 - lam_init))
    o_ref[0] = on.T.astype(BF16)


def _flash(stable, attn_layer, qt, k, vt, lq1, lk1, lq2, lk2, subln_col, lam_init):
    b, nk, d, tk = vt.shape
    _, s, _ = k.shape
    assert qt.shape == vt.shape and nk % Q_TILES == 0 and KV_BLOCKS_PER_STEP % Q_TILES == 0
    nq = nk // Q_TILES
    tq = Q_TILES * tk
    h = DIFF_HEADS
    hw = d // h
    hd = hw // 2
    vec = pl.BlockSpec((None, 1, hd), lambda bb, hh, ii, st: (attn_layer, 0, 0))
    grid_spec = pltpu.PrefetchScalarGridSpec(
        num_scalar_prefetch=1,
        grid=(b, h, nq),
        in_specs=[
            pl.BlockSpec((1, Q_TILES, hw, tk), lambda bb, hh, ii, st: (bb, ii, hh, 0)),
            pl.BlockSpec((1, s, hw), lambda bb, hh, ii, st: (bb, 0, hh)),
            pl.BlockSpec((1, nk, hw, tk), lambda bb, hh, ii, st: (bb, 0, hh, 0)),
            vec, vec, vec, vec,
            pl.BlockSpec((None, hw, 1), lambda bb, hh, ii, st: (attn_layer, 0, 0)),
        ],
        out_specs=pl.BlockSpec((1, tq, hw), lambda bb, hh, ii, st: (bb, ii, hh)),
        scratch_shapes=[
            pltpu.VMEM((2, hw, tq), BF16),
            pltpu.VMEM((2, 1, tq), F32),
            pltpu.VMEM((2, 1, tq), F32),
            pltpu.VMEM((2, hw, tq), F32),
        ],
    )
    return pl.pallas_call(
        functools.partial(_flash_kernel, hd=hd, lam_init=lam_init),
        out_shape=jax.ShapeDtypeStruct((b, s, d), BF16),
        grid_spec=grid_spec,
        compiler_params=_params(3),
        name="flash",
    )(stable, qt, k, vt, lq1, lk1, lq2, lk2, subln_col)


def _gelu(z):
    return 0.5 * z * (1.0 + lax.erf(z * math.sqrt(0.5)))


def _gmlp_kernel(x_ref, g_ref, win_ref, bin_ref, lng_ref, lnb_ref, ws_ref, bs_ref, wout_ref, bout_ref,
                 y_ref, v_scr, gated_scr, *, half):
    tm = x_ref.shape[0]
    gw = half // GMLP_GROUPS
    xn = _rms_rows(x_ref[...], g_ref[...]).astype(BF16)

    rsum = jnp.zeros((tm, 1), F32)
    rsq = jnp.zeros((tm, 1), F32)
    for g in range(GMLP_GROUPS):
        lo = half + g * gw
        z = jnp.dot(xn, win_ref[:, lo:lo + gw], preferred_element_type=F32) + bin_ref[:, lo:lo + gw]
        v = _gelu(z)
        v_scr[:, g * gw:(g + 1) * gw] = v
        rsum = rsum + jnp.sum(v, axis=-1, keepdims=True)
        rsq = rsq + jnp.sum(v * v, axis=-1, keepdims=True)
    mu = rsum * (1.0 / half)
    var = rsq * (1.0 / half) - mu * mu
    rstd = lax.rsqrt(var + LN_EPS)

    t_idx = lax.broadcasted_iota(jnp.int32, (GMLP_CHUNK, GMLP_CHUNK), 0)
    s_idx = lax.broadcasted_iota(jnp.int32, (GMLP_CHUNK, GMLP_CHUNK), 1)
    causal = s_idx <= t_idx

    for g in range(GMLP_GROUPS):
        lo = g * gw
        vn = ((v_scr[:, lo:lo + gw] - mu) * rstd * lng_ref[:, lo:lo + gw] + lnb_ref[:, lo:lo + gw]).astype(BF16)
        w = jnp.where(causal, ws_ref[g], 0.0).astype(BF16)
        z = jnp.dot(xn, win_ref[:, lo:lo + gw], preferred_element_type=F32) + bin_ref[:, lo:lo + gw]
        u = _gelu(z)
        for c in range(tm // GMLP_CHUNK):
            r = c * GMLP_CHUNK
            s = jnp.dot(w, vn[r:r + GMLP_CHUNK, :], preferred_element_type=F32) + bs_ref[:, lo:lo + gw]
            gated_scr[r:r + GMLP_CHUNK, lo:lo + gw] = (u[r:r + GMLP_CHUNK, :] * s).astype(BF16)

    y = jnp.dot(gated_scr[...], wout_ref[...], preferred_element_type=F32)
    y_ref[...] = x_ref[...] + y + bout_ref[...]


def _gmlp(x, layer, gains, gl, w_in, b_in, ln_g, ln_b, w_s, bs_full, w_out, b_out):
    t, d = x.shape
    half = w_out.shape[1]
    tm = GMLP_TILE
    assert tm % GMLP_CHUNK == 0 and t % tm == 0
    return pl.pallas_call(
        functools.partial(_gmlp_kernel, half=half),
        out_shape=jax.ShapeDtypeStruct((t, d), F32),
        grid=(t // tm,),
        in_specs=[
            pl.BlockSpec((tm, d), lambda i: (i, 0)),
            _layer_spec(layer, (1, d)),
            _layer_spec(gl, (d, 2 * half)),
            _layer_spec(gl, (1, 2 * half)),
            _layer_spec(gl, (1, half)),
            _layer_spec(gl, (1, half)),
            _layer_spec(gl, (GMLP_GROUPS, GMLP_CHUNK, GMLP_CHUNK)),
            _layer_spec(gl, (GMLP_CHUNK, half)),
            _layer_spec(gl, (half, d)),
            _layer_spec(gl, (1, d)),
        ],
        out_specs=pl.BlockSpec((tm, d), lambda i: (i, 0)),
        scratch_shapes=[pltpu.VMEM((tm, half), F32), pltpu.VMEM((tm, half), BF16)],
        compiler_params=_params(1),
        name="gmlp",
    )(x, gains, w_in, b_in, ln_g, ln_b, w_s, bs_full, w_out, b_out)


def _lambda_init(layer_idx):
    return 0.8 - 0.6 * math.exp(-0.3 * layer_idx)


def _rows(v):
    return v.reshape(v.shape[0], 1, v.shape[1]).astype(F32)


def kernel(x, ffn1_norm, ffn1_w_gate_up, ffn1_w_down, mix_norm, ffn2_norm, ffn2_w_gate_up, ffn2_w_down, attn_w_in, attn_w_out, attn_q_norm, attn_k_norm, attn_lambda_q1, attn_lambda_k1, attn_lambda_q2, attn_lambda_k2, attn_subln, gmlp_w_in, gmlp_b_in, gmlp_ln_g, gmlp_ln_b, gmlp_w_s, gmlp_b_s, gmlp_w_out, gmlp_b_out):
    b, s, d = x.shape
    depth = ffn1_norm.shape[0]
    h = DIFF_HEADS
    hd = d // h // 2
    nq = s // ATTN_TILE
    xf = x.reshape(b * s, d)

    ffn_gains = (_rows(ffn1_norm), _rows(ffn2_norm))
    ffn_f32 = ((ffn1_w_gate_up, ffn1_w_down), (ffn2_w_gate_up, ffn2_w_down))
    mix_gains = _rows(mix_norm)
    a_w_q_t = jnp.swapaxes(attn_w_in[:, :, :d], 1, 2).astype(BF16)
    a_w_k = attn_w_in[:, :, d:2 * d].astype(BF16)
    a_w_v_t = jnp.swapaxes(attn_w_in[:, :, 2 * d:], 1, 2).astype(BF16)
    a_w_out = attn_w_out.astype(BF16)
    feat_gain = attn_q_norm.astype(F32) * attn_k_norm.astype(F32) * (hd ** -0.5 * LOG2_E)
    kgain = _rows(jnp.tile(feat_gain, (1, 2 * h)))
    score_bound = 1.01 * hd * jnp.max(jnp.abs(feat_gain), axis=1)
    stable = (score_bound > SCORE_BOUND_LOG2).astype(jnp.int32)
    lam_vecs = [_rows(v) for v in (attn_lambda_q1, attn_lambda_k1, attn_lambda_q2, attn_lambda_k2)]
    subln_col = attn_subln.reshape(attn_subln.shape[0], -1, 1).astype(F32)
    half = gmlp_w_out.shape[1]
    g_params = (gmlp_w_in.astype(BF16), _rows(gmlp_b_in), _rows(gmlp_ln_g), _rows(gmlp_ln_b), gmlp_w_s.astype(F32),
                jnp.repeat(jnp.swapaxes(gmlp_b_s, 1, 2), half // GMLP_GROUPS, axis=2).astype(F32),
                gmlp_w_out.astype(BF16), _rows(gmlp_b_out))

    w_bf16 = [ffn1_w_gate_up[0:1].astype(BF16), ffn1_w_down[0:1].astype(BF16)]

    def ffn(xf, which, i, proj=None):
        nxt = (1, i) if which == 0 else (0, i + 1)
        cast = tuple((w, nxt[1]) for w in ffn_f32[nxt[0]]) if nxt[1] < depth else ()
        xf, *w_next = _ffn(xf, i, ffn_gains[which], *w_bf16, proj=proj, cast=cast, w_layer=0)
        w_bf16[:] = w_next
        return xf

    proj = None
    for i in range(depth):
        xf = ffn(xf, 0, i)
        j = i // 2
        if i % 2 == 0:
            k, qt, vt = _attn_in(xf, i, mix_gains, j, a_w_k, a_w_q_t, a_w_v_t, kgain, hd)
            o = _flash(stable[j:j + 1], j, qt.reshape(b, nq, d, ATTN_TILE), k.reshape(b, s, d),
                       vt.reshape(b, nq, d, ATTN_TILE), *lam_vecs, subln_col, _lambda_init(i))
            proj = (o.reshape(b * s, d), a_w_out, j)
        else:
            xf = _gmlp(xf, i, mix_gains, j, *g_params)
            proj = None
        xf = ffn(xf, 1, i, proj=proj)
    return xf.reshape(b, s, d)
```

```python
import functools
import math

import jax
import jax.numpy as jnp
from jax import lax
from jax.experimental import pallas as pl
from jax.experimental.pallas import tpu as pltpu

F32 = jnp.float32
BF16 = jnp.bfloat16

RMS_EPS = 1e-6
LN_EPS = 1e-5
DIFF_HEADS = 8
GMLP_GROUPS = 8
GMLP_CHUNK = 128

MXU_TILE = 256
VMEM_LIMIT_BYTES = 56 * 1024 * 1024
NEG_BIG = -1e30
LOG2_E = math.log2(math.e)
SCORE_BOUND_LOG2 = 60.0

GMLP_TILE = 1024
FFN_TILE = 1024
ATTN_TILE = 512
Q_TILES = 4
KV_BLOCKS_PER_STEP = 4


def _layer_spec(layer, shape, block_idx=None):
    block_idx = tuple(block_idx) if block_idx is not None else (0,) * len(shape)
    return pl.BlockSpec((None,) + tuple(shape), lambda *_: (layer,) + block_idx, pipeline_mode=pl.Buffered(1))


def _params(n_axes):
    return pltpu.CompilerParams(
        dimension_semantics=("arbitrary",) * n_axes,
        vmem_limit_bytes=VMEM_LIMIT_BYTES,
    )


def _rms_rows(x, gain):
    ms = jnp.mean(x * x, axis=-1, keepdims=True)
    return x * lax.rsqrt(ms + RMS_EPS) * gain


def _ffn_kernel(*refs, d_ff, with_proj, n_cast):
    refs = list(refs)
    x_ref = refs.pop(0)
    a_ref, wp_ref = (refs.pop(0), refs.pop(0)) if with_proj else (None, None)
    g_ref, wgu_ref, wd_ref = refs.pop(0), refs.pop(0), refs.pop(0)
    cast_src = [refs.pop(0) for _ in range(n_cast)]
    o_ref = refs.pop(0)
    cast_dst = [refs.pop(0) for _ in range(n_cast)]
    (h_ref,) = refs
    for src, dst in zip(cast_src, cast_dst):
        dst[...] = src[...].astype(BF16)
    if with_proj:
        o_ref[...] = x_ref[...] + jnp.dot(a_ref[...], wp_ref[...], preferred_element_type=F32)
        res_ref = o_ref
    else:
        res_ref = x_ref
    xn = _rms_rows(res_ref[...], g_ref[...]).astype(BF16)
    for j in range(d_ff // MXU_TILE):
        lo = j * MXU_TILE
        gate = jnp.dot(xn, wgu_ref[:, lo:lo + MXU_TILE], preferred_element_type=F32)
        up = jnp.dot(xn, wgu_ref[:, d_ff + lo:d_ff + lo + MXU_TILE], preferred_element_type=F32)
        h_ref[:, lo:lo + MXU_TILE] = (gate * jax.nn.sigmoid(gate) * up).astype(BF16)
    y = jnp.dot(h_ref[...], wd_ref[...], preferred_element_type=F32)
    o_ref[...] = res_ref[...] + 0.5 * y


def _ffn(x, layer, gains, w_gate_up, w_down, proj=None, cast=(), w_layer=0):
    t, d = x.shape
    d_ff = w_down.shape[1]
    assert d_ff % MXU_TILE == 0 and t % FFN_TILE == 0
    tm = FFN_TILE
    steps = t // tm
    row_spec = pl.BlockSpec((tm, d), lambda i: (i, 0))
    operands, in_specs = [x], [row_spec]
    if proj is not None:
        a, w_proj, proj_layer = proj
        operands += [a, w_proj]
        in_specs += [pl.BlockSpec((tm, a.shape[1]), lambda i: (i, 0)), _layer_spec(proj_layer, w_proj.shape[1:])]
    operands += [gains, w_gate_up, w_down]
    in_specs += [_layer_spec(layer, (1, d)), _layer_spec(w_layer, (d, 2 * d_ff)), _layer_spec(w_layer, (d_ff, d))]
    out_shape, out_specs = [jax.ShapeDtypeStruct((t, d), F32)], [row_spec]
    for stack, src_layer in cast:
        n_layers, rows, cols = stack.shape
        assert rows % steps == 0
        chunk = rows // steps
        operands.append(stack.reshape(n_layers, steps, chunk, cols))
        in_specs.append(pl.BlockSpec((None, None, chunk, cols), lambda i, src_layer=src_layer: (src_layer, i, 0, 0)))
        out_shape.append(jax.ShapeDtypeStruct((steps, chunk, cols), BF16))
        out_specs.append(pl.BlockSpec((None, chunk, cols), lambda i: (i, 0, 0)))
    outs = pl.pallas_call(
        functools.partial(_ffn_kernel, d_ff=d_ff, with_proj=proj is not None, n_cast=len(cast)),
        out_shape=out_shape,
        grid=(steps,),
        in_specs=in_specs,
        out_specs=out_specs,
        scratch_shapes=[pltpu.VMEM((tm, d_ff), BF16)],
        compiler_params=_params(1),
        name="ffn",
    )(*operands)
    return [outs[0]] + [o.reshape(1, stack.shape[1], stack.shape[2]) for o, (stack, _) in zip(outs[1:], cast)]


def _group_ones(n, group):
    r = lax.broadcasted_iota(jnp.int32, (n, n), 0) // group
    c = lax.broadcasted_iota(jnp.int32, (n, n), 1) // group
    return (r == c).astype(BF16)


def _attn_in_kernel(x_ref, g_ref, wk_ref, wqt_ref, wvt_ref, kgain_ref, k_ref, qt_ref, vt_ref, *, hd):
    d = x_ref.shape[1]
    xn = _rms_rows(x_ref[...], g_ref[...]).astype(BF16)
    ones = _group_ones(MXU_TILE, hd)
    nt_dims = (((1,), (1,)), ((), ()))

    k = jnp.dot(xn, wk_ref[...], preferred_element_type=F32)
    for j in range(d // MXU_TILE):
        lo = j * MXU_TILE
        kj = k[:, lo:lo + MXU_TILE]
        ss = jnp.dot((kj * kj).astype(BF16), ones, preferred_element_type=F32)
        kn = kj * lax.rsqrt(ss * (1.0 / hd) + RMS_EPS) * kgain_ref[:, lo:lo + MXU_TILE]
        k_ref[:, lo:lo + MXU_TILE] = kn.astype(BF16)

    qt = lax.dot_general(wqt_ref[...], xn, nt_dims, preferred_element_type=F32)
    for j in range(d // MXU_TILE):
        lo = j * MXU_TILE
        qj = qt[lo:lo + MXU_TILE, :]
        ss = jnp.dot(ones, (qj * qj).astype(BF16), preferred_element_type=F32)
        qt_ref[0, lo:lo + MXU_TILE, :] = (qj * lax.rsqrt(ss * (1.0 / hd) + RMS_EPS)).astype(BF16)

    vt = lax.dot_general(wvt_ref[...], xn, nt_dims, preferred_element_type=F32)
    vt_ref[0] = vt.astype(BF16)


def _attn_in(x, layer, gains, attn_layer, w_k, w_q_t, w_v_t, kgain, hd):
    t, d = x.shape
    tm = ATTN_TILE
    nt = t // tm
    return pl.pallas_call(
        functools.partial(_attn_in_kernel, hd=hd),
        out_shape=(
            jax.ShapeDtypeStruct((t, d), BF16),
            jax.ShapeDtypeStruct((nt, d, tm), BF16),
            jax.ShapeDtypeStruct((nt, d, tm), BF16),
        ),
        grid=(nt,),
        in_specs=[
            pl.BlockSpec((tm, d), lambda i: (i, 0)),
            _layer_spec(layer, (1, d)),
            _layer_spec(attn_layer, (d, d)),
            _layer_spec(attn_layer, (d, d)),
            _layer_spec(attn_layer, (d, d)),
            _layer_spec(attn_layer, (1, d)),
        ],
        out_specs=(
            pl.BlockSpec((tm, d), lambda i: (i, 0)),
            pl.BlockSpec((1, d, tm), lambda i: (i, 0, 0)),
            pl.BlockSpec((1, d, tm), lambda i: (i, 0, 0)),
        ),
        compiler_params=_params(1),
        name="attn_in",
    )(x, gains, w_k, w_q_t, w_v_t, kgain)


def _flash_kernel(stable_ref, qt_ref, k_ref, vt_ref, lq1_ref, lk1_ref, lq2_ref, lk2_ref, sub_ref, o_ref,
                  q_scr, m_scr, l_scr, acc_scr, *, hd, lam_init):
    i = pl.program_id(2)
    tk = vt_ref.shape[3]
    tq = Q_TILES * tk

    qt = jnp.concatenate([qt_ref[0, n] for n in range(Q_TILES)], axis=1)
    row = lax.broadcasted_iota(jnp.int32, qt.shape, 0)
    zero = jnp.zeros_like(qt)
    q_scr[0] = jnp.where(row < hd, qt, zero)
    q_scr[1] = jnp.where(row >= hd, qt, zero)
    m_scr[...] = jnp.full(m_scr.shape, NEG_BIG, F32)
    l_scr[...] = jnp.zeros(l_scr.shape, F32)
    acc_scr[...] = jnp.zeros(acc_scr.shape, F32)

    def kv_block(j, nblk, q_lo, diagonal, stable):
        start = pl.multiple_of(j * tk, tk)
        kj = k_ref[0, pl.ds(start, nblk * tk), :]
        vj = jnp.concatenate([vt_ref[0, j + n] for n in range(nblk)], axis=1)
        for c in range(2):
            s = jnp.dot(kj, q_scr[c, :, q_lo:], preferred_element_type=F32)
            if diagonal:
                kpos = lax.broadcasted_iota(jnp.int32, s.shape, 0) - (nblk - 1) * tk
                qpos = lax.broadcasted_iota(jnp.int32, s.shape, 1)
                s = jnp.where(kpos <= qpos, s, NEG_BIG)
            if stable:
                m_old = m_scr[c, :, q_lo:]
                m_new = jnp.maximum(m_old, jnp.max(s, axis=0, keepdims=True))
                alpha = jnp.exp2(m_old - m_new)
                p = jnp.exp2(s - m_new)
                l_scr[c, :, q_lo:] = alpha * l_scr[c, :, q_lo:] + jnp.sum(p, axis=0, keepdims=True)
                acc_scr[c, :, q_lo:] = (alpha * acc_scr[c, :, q_lo:]
                                        + jnp.dot(vj, p.astype(BF16), preferred_element_type=F32))
                m_scr[c, :, q_lo:] = m_new
            else:
                p = jnp.exp2(s)
                l_scr[c, :, q_lo:] = l_scr[c, :, q_lo:] + jnp.sum(p, axis=0, keepdims=True)
                acc_scr[c, :, q_lo:] = acc_scr[c, :, q_lo:] + jnp.dot(vj, p.astype(BF16), preferred_element_type=F32)

    def run(stable, chunk):
        n_below = Q_TILES * i
        n_trips = n_below // chunk

        def body(jj, carry):
            kv_block(jj * chunk, chunk, 0, diagonal=False, stable=stable)
            return carry
        lax.fori_loop(0, n_trips, body, 0)

        for rem in range(0, chunk, Q_TILES):
            @pl.when(n_below - n_trips * chunk == rem)
            def _():
                kv_block(n_trips * chunk, rem + 1, 0, diagonal=True, stable=stable)
                for n in range(1, Q_TILES):
                    kv_block(n_below + n, 1, n * tk, diagonal=True, stable=stable)

    @pl.when(stable_ref[0] == 0)
    def _():
        run(stable=False, chunk=KV_BLOCKS_PER_STEP)

    @pl.when(stable_ref[0] != 0)
    def _():
        run(stable=True, chunk=1)

    lam = (jnp.exp(jnp.sum(lq1_ref[...] * lk1_ref[...], axis=-1, keepdims=True))
           - jnp.exp(jnp.sum(lq2_ref[...] * lk2_ref[...], axis=-1, keepdims=True))
           + lam_init)
    o = acc_scr[0] * (1.0 / l_scr[0]) - acc_scr[1] * (lam / l_scr[1])
    ms = jnp.mean(o * o, axis=0, keepdims=True)
    on = o * lax.rsqrt(ms + RMS_EPS) * (sub_ref[...] * (1.0 - lam_init))
    o_ref[0] = on.T.astype(BF16)


def _flash(stable, attn_layer, qt, k, vt, lq1, lk1, lq2, lk2, subln_col, lam_init):
    b, nk, d, tk = vt.shape
    _, s, _ = k.shape
    assert qt.shape == vt.shape and nk % Q_TILES == 0 and KV_BLOCKS_PER_STEP % Q_TILES == 0
    nq = nk // Q_TILES
    tq = Q_TILES * tk
    h = DIFF_HEADS
    hw = d // h
    hd = hw // 2
    vec = pl.BlockSpec((None, 1, hd), lambda bb, hh, ii, st: (attn_layer, 0, 0))
    grid_spec = pltpu.PrefetchScalarGridSpec(
        num_scalar_prefetch=1,
        grid=(b, h, nq),
        in_specs=[
            pl.BlockSpec((1, Q_TILES, hw, tk), lambda bb, hh, ii, st: (bb, ii, hh, 0)),
            pl.BlockSpec((1, s, hw), lambda bb, hh, ii, st: (bb, 0, hh)),
            pl.BlockSpec((1, nk, hw, tk), lambda bb, hh, ii, st: (bb, 0, hh, 0)),
            vec, vec, vec, vec,
            pl.BlockSpec((None, hw, 1), lambda bb, hh, ii, st: (attn_layer, 0, 0)),
        ],
        out_specs=pl.BlockSpec((1, tq, hw), lambda bb, hh, ii, st: (bb, ii, hh)),
        scratch_shapes=[
            pltpu.VMEM((2, hw, tq), BF16),
            pltpu.VMEM((2, 1, tq), F32),
            pltpu.VMEM((2, 1, tq), F32),
            pltpu.VMEM((2, hw, tq), F32),
        ],
    )
    return pl.pallas_call(
        functools.partial(_flash_kernel, hd=hd, lam_init=lam_init),
        out_shape=jax.ShapeDtypeStruct((b, s, d), BF16),
        grid_spec=grid_spec,
        compiler_params=_params(3),
        name="flash",
    )(stable, qt, k, vt, lq1, lk1, lq2, lk2, subln_col)


def _gelu(z):
    return 0.5 * z * (1.0 + lax.erf(z * math.sqrt(0.5)))


def _gmlp_kernel(x_ref, g_ref, win_ref, bin_ref, lng_ref, lnb_ref, ws_ref, bs_ref, wout_ref, bout_ref,
                 y_ref, v_scr, gated_scr, *, half):
    tm = x_ref.shape[0]
    gw = half // GMLP_GROUPS
    xn = _rms_rows(x_ref[...], g_ref[...]).astype(BF16)

    rsum = jnp.zeros((tm, 1), F32)
    rsq = jnp.zeros((tm, 1), F32)
    for g in range(GMLP_GROUPS):
        lo = half + g * gw
        z = jnp.dot(xn, win_ref[:, lo:lo + gw], preferred_element_type=F32) + bin_ref[:, lo:lo + gw]
        v = _gelu(z)
        v_scr[:, g * gw:(g + 1) * gw] = v
        rsum = rsum + jnp.sum(v, axis=-1, keepdims=True)
        rsq = rsq + jnp.sum(v * v, axis=-1, keepdims=True)
    mu = rsum * (1.0 / half)
    var = rsq * (1.0 / half) - mu * mu
    rstd = lax.rsqrt(var + LN_EPS)

    t_idx = lax.broadcasted_iota(jnp.int32, (GMLP_CHUNK, GMLP_CHUNK), 0)
    s_idx = lax.broadcasted_iota(jnp.int32, (GMLP_CHUNK, GMLP_CHUNK), 1)
    causal = s_idx <= t_idx

    for g in range(GMLP_GROUPS):
        lo = g * gw
        vn = ((v_scr[:, lo:lo + gw] - mu) * rstd * lng_ref[:, lo:lo + gw] + lnb_ref[:, lo:lo + gw]).astype(BF16)
        w = jnp.where(causal, ws_ref[g], 0.0).astype(BF16)
        z = jnp.dot(xn, win_ref[:, lo:lo + gw], preferred_element_type=F32) + bin_ref[:, lo:lo + gw]
        u = _gelu(z)
        for c in range(tm // GMLP_CHUNK):
            r = c * GMLP_CHUNK
            s = jnp.dot(w, vn[r:r + GMLP_CHUNK, :], preferred_element_type=F32) + bs_ref[:, lo:lo + gw]
            gated_scr[r:r + GMLP_CHUNK, lo:lo + gw] = (u[r:r + GMLP_CHUNK, :] * s).astype(BF16)

    y = jnp.dot(gated_scr[...], wout_ref[...], preferred_element_type=F32)
    y_ref[...] = x_ref[...] + y + bout_ref[...]


def _gmlp(x, layer, gains, gl, wl, w_in, b_in, ln_g, ln_b, w_s, bs_full, w_out, b_out):
    t, d = x.shape
    half = w_out.shape[1]
    tm = GMLP_TILE
    assert tm % GMLP_CHUNK == 0 and t % tm == 0
    return pl.pallas_call(
        functools.partial(_gmlp_kernel, half=half),
        out_shape=jax.ShapeDtypeStruct((t, d), F32),
        grid=(t // tm,),
        in_specs=[
            pl.BlockSpec((tm, d), lambda i: (i, 0)),
            _layer_spec(layer, (1, d)),
            _layer_spec(wl, (d, 2 * half)),
            _layer_spec(gl, (1, 2 * half)),
            _layer_spec(gl, (1, half)),
            _layer_spec(gl, (1, half)),
            _layer_spec(gl, (GMLP_GROUPS, GMLP_CHUNK, GMLP_CHUNK)),
            _layer_spec(gl, (GMLP_CHUNK, half)),
            _layer_spec(wl, (half, d)),
            _layer_spec(gl, (1, d)),
        ],
        out_specs=pl.BlockSpec((tm, d), lambda i: (i, 0)),
        scratch_shapes=[pltpu.VMEM((tm, half), F32), pltpu.VMEM((tm, half), BF16)],
        compiler_params=_params(1),
        name="gmlp",
    )(x, gains, w_in, b_in, ln_g, ln_b, w_s, bs_full, w_out, b_out)


def _lambda_init(layer_idx):
    return 0.8 - 0.6 * math.exp(-0.3 * layer_idx)


def _rows(v):
    return v.reshape(v.shape[0], 1, v.shape[1]).astype(F32)


def kernel(x, ffn1_norm, ffn1_w_gate_up, ffn1_w_down, mix_norm, ffn2_norm, ffn2_w_gate_up, ffn2_w_down, attn_w_in, attn_w_out, attn_q_norm, attn_k_norm, attn_lambda_q1, attn_lambda_k1, attn_lambda_q2, attn_lambda_k2, attn_subln, gmlp_w_in, gmlp_b_in, gmlp_ln_g, gmlp_ln_b, gmlp_w_s, gmlp_b_s, gmlp_w_out, gmlp_b_out):
    b, s, d = x.shape
    depth = ffn1_norm.shape[0]
    h = DIFF_HEADS
    hd = d // h // 2
    nq = s // ATTN_TILE
    xf = x.reshape(b * s, d)

    ffn_gains = (_rows(ffn1_norm), _rows(ffn2_norm))
    ffn_f32 = ((ffn1_w_gate_up, ffn1_w_down), (ffn2_w_gate_up, ffn2_w_down))
    mix_gains = _rows(mix_norm)
    a_w_q_t = jnp.swapaxes(attn_w_in[:, :, :d], 1, 2).astype(BF16)
    a_w_k = attn_w_in[:, :, d:2 * d].astype(BF16)
    a_w_v_t = jnp.swapaxes(attn_w_in[:, :, 2 * d:], 1, 2).astype(BF16)
    a_w_out = attn_w_out.astype(BF16)
    feat_gain = attn_q_norm.astype(F32) * attn_k_norm.astype(F32) * (hd ** -0.5 * LOG2_E)
    kgain = _rows(jnp.tile(feat_gain, (1, 2 * h)))
    score_bound = 1.01 * hd * jnp.max(jnp.abs(feat_gain), axis=1)
    stable = (score_bound > SCORE_BOUND_LOG2).astype(jnp.int32)
    lam_vecs = [_rows(v) for v in (attn_lambda_q1, attn_lambda_k1, attn_lambda_q2, attn_lambda_k2)]
    subln_col = attn_subln.reshape(attn_subln.shape[0], -1, 1).astype(F32)
    half = gmlp_w_out.shape[1]
    g_vecs = (_rows(gmlp_b_in), _rows(gmlp_ln_g), _rows(gmlp_ln_b), gmlp_w_s.astype(F32),
              jnp.repeat(jnp.swapaxes(gmlp_b_s, 1, 2), half // GMLP_GROUPS, axis=2).astype(F32))

    w_bf16 = [ffn1_w_gate_up[0:1].astype(BF16), ffn1_w_down[0:1].astype(BF16)]

    def ffn(xf, which, i, proj=None, extra=()):
        nxt = (1, i) if which == 0 else (0, i + 1)
        cast = tuple((w, nxt[1]) for w in ffn_f32[nxt[0]]) if nxt[1] < depth else ()
        xf, *w_next = _ffn(xf, i, ffn_gains[which], *w_bf16, proj=proj, cast=cast + tuple(extra), w_layer=0)
        w_bf16[:] = w_next[:len(cast)]
        return xf, w_next[len(cast):]

    for i in range(depth):
        j = i // 2
        if i % 2 == 0:
            xf, _ = ffn(xf, 0, i)
            k, qt, vt = _attn_in(xf, i, mix_gains, j, a_w_k, a_w_q_t, a_w_v_t, kgain, hd)
            o = _flash(stable[j:j + 1], j, qt.reshape(b, nq, d, ATTN_TILE), k.reshape(b, s, d),
                       vt.reshape(b, nq, d, ATTN_TILE), *lam_vecs, subln_col, _lambda_init(i))
            proj = (o.reshape(b * s, d), a_w_out, j)
        else:
            xf, (g_w_in, g_w_out) = ffn(xf, 0, i, extra=((gmlp_w_in, j), (gmlp_w_out, j)))
            xf = _gmlp(xf, i, mix_gains, j, 0, g_w_in, *g_vecs, g_w_out, _rows(gmlp_b_out))
            proj = None
        xf, _ = ffn(xf, 1, i, proj=proj)
    return xf.reshape(b, s, d)
```

```python
import functools
import math

import jax
import jax.numpy as jnp
from jax import lax
from jax.experimental import pallas as pl
from jax.experimental.pallas import tpu as pltpu

F32 = jnp.float32
BF16 = jnp.bfloat16

RMS_EPS = 1e-6
LN_EPS = 1e-5
DIFF_HEADS = 8
GMLP_GROUPS = 8
GMLP_CHUNK = 128

MXU_TILE = 256
VMEM_LIMIT_BYTES = 56 * 1024 * 1024
NEG_BIG = -1e30
LOG2_E = math.log2(math.e)
SCORE_BOUND_LOG2 = 32.0

GMLP_TILE = 1024
FFN_TILE = 1024
ATTN_TILE = 512
ATTN_IN_TILES = 2
Q_TILES = 4
KV_BLOCKS_PER_STEP = 4


def _layer_spec(layer, shape, block_idx=None):
    block_idx = tuple(block_idx) if block_idx is not None else (0,) * len(shape)
    return pl.BlockSpec((None,) + tuple(shape), lambda *_: (layer,) + block_idx, pipeline_mode=pl.Buffered(1))


def _params(n_axes):
    return pltpu.CompilerParams(
        dimension_semantics=("arbitrary",) * n_axes,
        vmem_limit_bytes=VMEM_LIMIT_BYTES,
    )


def _rms_rows(x, gain):
    ms = jnp.mean(x * x, axis=-1, keepdims=True)
    return x * lax.rsqrt(ms + RMS_EPS) * gain


def _ffn_kernel(*refs, d_ff, with_proj, n_cast):
    refs = list(refs)
    x_ref = refs.pop(0)
    a_ref, wp_ref = (refs.pop(0), refs.pop(0)) if with_proj else (None, None)
    g_ref, wgu_ref, wd_ref = refs.pop(0), refs.pop(0), refs.pop(0)
    cast_src = [refs.pop(0) for _ in range(n_cast)]
    o_ref = refs.pop(0)
    cast_dst = [refs.pop(0) for _ in range(n_cast)]
    (h_ref,) = refs
    for src, dst in zip(cast_src, cast_dst):
        dst[...] = src[...].astype(BF16)
    if with_proj:
        o_ref[...] = x_ref[...] + jnp.dot(a_ref[...], wp_ref[...], preferred_element_type=F32)
        res_ref = o_ref
    else:
        res_ref = x_ref
    xn = _rms_rows(res_ref[...], g_ref[...]).astype(BF16)
    for j in range(d_ff // MXU_TILE):
        lo = j * MXU_TILE
        gate = jnp.dot(xn, wgu_ref[:, lo:lo + MXU_TILE], preferred_element_type=F32)
        up = jnp.dot(xn, wgu_ref[:, d_ff + lo:d_ff + lo + MXU_TILE], preferred_element_type=F32)
        h_ref[:, lo:lo + MXU_TILE] = (gate * jax.nn.sigmoid(gate) * up).astype(BF16)
    y = jnp.dot(h_ref[...], wd_ref[...], preferred_element_type=F32)
    o_ref[...] = res_ref[...] + 0.5 * y


def _ffn(x, layer, gains, w_gate_up, w_down, proj=None, cast=(), w_layer=0):
    t, d = x.shape
    d_ff = w_down.shape[1]
    assert d_ff % MXU_TILE == 0 and t % FFN_TILE == 0
    tm = FFN_TILE
    steps = t // tm
    row_spec = pl.BlockSpec((tm, d), lambda i: (i, 0))
    operands, in_specs = [x], [row_spec]
    if proj is not None:
        a, w_proj, proj_layer = proj
        operands += [a, w_proj]
        in_specs += [pl.BlockSpec((tm, a.shape[1]), lambda i: (i, 0)), _layer_spec(proj_layer, w_proj.shape[1:])]
    operands += [gains, w_gate_up, w_down]
    in_specs += [_layer_spec(layer, (1, d)), _layer_spec(w_layer, (d, 2 * d_ff)), _layer_spec(w_layer, (d_ff, d))]
    out_shape, out_specs = [jax.ShapeDtypeStruct((t, d), F32)], [row_spec]
    for stack, src_layer in cast:
        n_layers, rows, cols = stack.shape
        assert rows % steps == 0
        chunk = rows // steps
        operands.append(stack.reshape(n_layers, steps, chunk, cols))
        in_specs.append(pl.BlockSpec((None, None, chunk, cols), lambda i, src_layer=src_layer: (src_layer, i, 0, 0)))
        out_shape.append(jax.ShapeDtypeStruct((steps, chunk, cols), BF16))
        out_specs.append(pl.BlockSpec((None, chunk, cols), lambda i: (i, 0, 0)))
    outs = pl.pallas_call(
        functools.partial(_ffn_kernel, d_ff=d_ff, with_proj=proj is not None, n_cast=len(cast)),
        out_shape=out_shape,
        grid=(steps,),
        in_specs=in_specs,
        out_specs=out_specs,
        scratch_shapes=[pltpu.VMEM((tm, d_ff), BF16)],
        compiler_params=_params(1),
        name="ffn",
    )(*operands)
    return [outs[0]] + [o.reshape(1, stack.shape[1], stack.shape[2]) for o, (stack, _) in zip(outs[1:], cast)]


def _group_ones(n, group):
    r = lax.broadcasted_iota(jnp.int32, (n, n), 0) // group
    c = lax.broadcasted_iota(jnp.int32, (n, n), 1) // group
    return (r == c).astype(BF16)


def _attn_in_kernel(x_ref, g_ref, wk_ref, wqt_ref, wvt_ref, kgain_ref, k_ref, qt_ref, vt_ref, *, hd):
    d = x_ref.shape[1]
    xn = _rms_rows(x_ref[...], g_ref[...]).astype(BF16)
    ones = _group_ones(MXU_TILE, hd)
    nt_dims = (((1,), (1,)), ((), ()))

    k = jnp.dot(xn, wk_ref[...], preferred_element_type=F32)
    for j in range(d // MXU_TILE):
        lo = j * MXU_TILE
        kj = k[:, lo:lo + MXU_TILE]
        ss = jnp.dot((kj * kj).astype(BF16), ones, preferred_element_type=F32)
        kn = kj * lax.rsqrt(ss * (1.0 / hd) + RMS_EPS) * kgain_ref[:, lo:lo + MXU_TILE]
        k_ref[:, lo:lo + MXU_TILE] = kn.astype(BF16)

    qt = lax.dot_general(wqt_ref[...], xn, nt_dims, preferred_element_type=F32)
    qg = qt.reshape(d // hd, hd, qt.shape[1])
    ms = jnp.mean(qg * qg, axis=1, keepdims=True)
    qn = (qg * lax.rsqrt(ms + RMS_EPS)).reshape(qt.shape).astype(BF16)

    vt = lax.dot_general(wvt_ref[...], xn, nt_dims, preferred_element_type=F32).astype(BF16)
    tk = qt_ref.shape[2]
    for n in range(qt_ref.shape[0]):
        qt_ref[n] = qn[:, n * tk:(n + 1) * tk]
        vt_ref[n] = vt[:, n * tk:(n + 1) * tk]


def _attn_in(x, layer, gains, attn_layer, w_k, w_q_t, w_v_t, kgain, hd):
    t, d = x.shape
    per_step = ATTN_IN_TILES
    tm = per_step * ATTN_TILE
    nt = t // ATTN_TILE
    tile_major = pl.BlockSpec((per_step, d, ATTN_TILE), lambda i: (i, 0, 0))
    return pl.pallas_call(
        functools.partial(_attn_in_kernel, hd=hd),
        out_shape=(
            jax.ShapeDtypeStruct((t, d), BF16),
            jax.ShapeDtypeStruct((nt, d, ATTN_TILE), BF16),
            jax.ShapeDtypeStruct((nt, d, ATTN_TILE), BF16),
        ),
        grid=(t // tm,),
        in_specs=[
            pl.BlockSpec((tm, d), lambda i: (i, 0)),
            _layer_spec(layer, (1, d)),
            _layer_spec(attn_layer, (d, d)),
            _layer_spec(attn_layer, (d, d)),
            _layer_spec(attn_layer, (d, d)),
            _layer_spec(attn_layer, (1, d)),
        ],
        out_specs=(pl.BlockSpec((tm, d), lambda i: (i, 0)), tile_major, tile_major),
        compiler_params=_params(1),
        name="attn_in",
    )(x, gains, w_k, w_q_t, w_v_t, kgain)


def _flash_kernel(stable_ref, qt_ref, k_ref, vt_ref, lq1_ref, lk1_ref, lq2_ref, lk2_ref, sub_ref, o_ref,
                  q_scr, m_scr, l_scr, acc_scr, *, hd, lam_init):
    i = pl.program_id(2)
    tk = vt_ref.shape[3]
    tq = Q_TILES * tk

    qt = jnp.concatenate([qt_ref[0, n] for n in range(Q_TILES)], axis=1)
    row = lax.broadcasted_iota(jnp.int32, qt.shape, 0)
    zero = jnp.zeros_like(qt)
    q_scr[0] = jnp.where(row < hd, qt, zero)
    q_scr[1] = jnp.where(row >= hd, qt, zero)
    m_scr[...] = jnp.full(m_scr.shape, NEG_BIG, F32)
    l_scr[...] = jnp.zeros(l_scr.shape, F32)
    acc_scr[...] = jnp.zeros(acc_scr.shape, F32)

    def kv_block(j, nblk, q_lo, diagonal, stable):
        start = pl.multiple_of(j * tk, tk)
        kj = k_ref[0, pl.ds(start, nblk * tk), :]
        vj = jnp.concatenate([vt_ref[0, j + n] for n in range(nblk)], axis=1)
        for c in range(2):
            s = jnp.dot(kj, q_scr[c, :, q_lo:], preferred_element_type=F32)
            if diagonal:
                kpos = lax.broadcasted_iota(jnp.int32, s.shape, 0) - (nblk - 1) * tk
                qpos = lax.broadcasted_iota(jnp.int32, s.shape, 1)
                s = jnp.where(kpos <= qpos, s, NEG_BIG)
            if stable:
                m_old = m_scr[c, :, q_lo:]
                m_new = jnp.maximum(m_old, jnp.max(s, axis=0, keepdims=True))
                alpha = jnp.exp2(m_old - m_new)
                p = jnp.exp2(s - m_new)
                l_scr[c, :, q_lo:] = alpha * l_scr[c, :, q_lo:] + jnp.sum(p, axis=0, keepdims=True)
                acc_scr[c, :, q_lo:] = (alpha * acc_scr[c, :, q_lo:]
                                        + jnp.dot(vj, p.astype(BF16), preferred_element_type=F32))
                m_scr[c, :, q_lo:] = m_new
            else:
                p = jnp.exp2(s)
                l_scr[c, :, q_lo:] = l_scr[c, :, q_lo:] + jnp.sum(p, axis=0, keepdims=True)
                acc_scr[c, :, q_lo:] = acc_scr[c, :, q_lo:] + jnp.dot(vj, p.astype(BF16), preferred_element_type=F32)

    def run(stable, chunk):
        n_below = Q_TILES * i
        n_trips = n_below // chunk

        def body(jj, carry):
            kv_block(jj * chunk, chunk, 0, diagonal=False, stable=stable)
            return carry
        lax.fori_loop(0, n_trips, body, 0)

        for rem in range(0, chunk, Q_TILES):
            @pl.when(n_below - n_trips * chunk == rem)
            def _():
                kv_block(n_trips * chunk, rem + 1, 0, diagonal=True, stable=stable)
                for n in range(1, Q_TILES):
                    kv_block(n_below + n, 1, n * tk, diagonal=True, stable=stable)

    @pl.when(stable_ref[0] == 0)
    def _():
        run(stable=False, chunk=KV_BLOCKS_PER_STEP)

    @pl.when(stable_ref[0] != 0)
    def _():
        run(stable=True, chunk=1)

    lam = (jnp.exp(jnp.sum(lq1_ref[...] * lk1_ref[...], axis=-1, keepdims=True))
           - jnp.exp(jnp.sum(lq2_ref[...] * lk2_ref[...], axis=-1, keepdims=True))
           + lam_init)
    o = acc_scr[0] * (1.0 / l_scr[0]) - acc_scr[1] * (lam / l_scr[1])
    ms = jnp.mean(o * o, axis=0, keepdims=True)
    on = o * lax.rsqrt(ms + RMS_EPS) * (sub_ref[...] * (1.0 - lam_init))
    o_ref[0] = on.T.astype(BF16)


def _flash(stable, attn_layer, qt, k, vt, lq1, lk1, lq2, lk2, subln_col, lam_init):
    b, nk, d, tk = vt.shape
    _, s, _ = k.shape
    assert qt.shape == vt.shape and nk % Q_TILES == 0 and KV_BLOCKS_PER_STEP % Q_TILES == 0
    nq = nk // Q_TILES
    tq = Q_TILES * tk
    h = DIFF_HEADS
    hw = d // h
    hd = hw // 2
    vec = pl.BlockSpec((None, 1, hd), lambda bb, hh, ii, st: (attn_layer, 0, 0))
    grid_spec = pltpu.PrefetchScalarGridSpec(
        num_scalar_prefetch=1,
        grid=(b, h, nq),
        in_specs=[
            pl.BlockSpec((1, Q_TILES, hw, tk), lambda bb, hh, ii, st: (bb, ii, hh, 0)),
            pl.BlockSpec((1, s, hw), lambda bb, hh, ii, st: (bb, 0, hh)),
            pl.BlockSpec((1, nk, hw, tk), lambda bb, hh, ii, st: (bb, 0, hh, 0)),
            vec, vec, vec, vec,
            pl.BlockSpec((None, hw, 1), lambda bb, hh, ii, st: (attn_layer, 0, 0)),
        ],
        out_specs=pl.BlockSpec((1, tq, hw), lambda bb, hh, ii, st: (bb, ii, hh)),
        scratch_shapes=[
            pltpu.VMEM((2, hw, tq), BF16),
            pltpu.VMEM((2, 1, tq), F32),
            pltpu.VMEM((2, 1, tq), F32),
            pltpu.VMEM((2, hw, tq), F32),
        ],
    )
    return pl.pallas_call(
        functools.partial(_flash_kernel, hd=hd, lam_init=lam_init),
        out_shape=jax.ShapeDtypeStruct((b, s, d), BF16),
        grid_spec=grid_spec,
        compiler_params=_params(3),
        name="flash",
    )(stable, qt, k, vt, lq1, lk1, lq2, lk2, subln_col)


def _gelu(z):
    return 0.5 * z * (1.0 + lax.erf(z * math.sqrt(0.5)))


def _gmlp_kernel(x_ref, g_ref, win_ref, bin_ref, lng_ref, lnb_ref, ws_ref, bs_ref, wout_ref, bout_ref,
                 y_ref, v_scr, gated_scr, *, half):
    tm = x_ref.shape[0]
    gw = half // GMLP_GROUPS
    xn = _rms_rows(x_ref[...], g_ref[...]).astype(BF16)

    rsum = jnp.zeros((tm, 1), F32)
    rsq = jnp.zeros((tm, 1), F32)
    for g in range(GMLP_GROUPS):
        lo = half + g * gw
        z = jnp.dot(xn, win_ref[:, lo:lo + gw], preferred_element_type=F32) + bin_ref[:, lo:lo + gw]
        v = _gelu(z)
        v_scr[:, g * gw:(g + 1) * gw] = v
        rsum = rsum + jnp.sum(v, axis=-1, keepdims=True)
        rsq = rsq + jnp.sum(v * v, axis=-1, keepdims=True)
    mu = rsum * (1.0 / half)
    var = rsq * (1.0 / half) - mu * mu
    rstd = lax.rsqrt(var + LN_EPS)

    t_idx = lax.broadcasted_iota(jnp.int32, (GMLP_CHUNK, GMLP_CHUNK), 0)
    s_idx = lax.broadcasted_iota(jnp.int32, (GMLP_CHUNK, GMLP_CHUNK), 1)
    causal = s_idx <= t_idx

    for g in range(GMLP_GROUPS):
        lo = g * gw
        vn = ((v_scr[:, lo:lo + gw] - mu) * rstd * lng_ref[:, lo:lo + gw] + lnb_ref[:, lo:lo + gw]).astype(BF16)
        w = jnp.where(causal, ws_ref[g], 0.0).astype(BF16)
        z = jnp.dot(xn, win_ref[:, lo:lo + gw], preferred_element_type=F32) + bin_ref[:, lo:lo + gw]
        u = _gelu(z)
        for c in range(tm // GMLP_CHUNK):
            r = c * GMLP_CHUNK
            s = jnp.dot(w, vn[r:r + GMLP_CHUNK, :], preferred_element_type=F32) + bs_ref[:, lo:lo + gw]
            gated_scr[r:r + GMLP_CHUNK, lo:lo + gw] = (u[r:r + GMLP_CHUNK, :] * s).astype(BF16)

    y = jnp.dot(gated_scr[...], wout_ref[...], preferred_element_type=F32)
    y_ref[...] = x_ref[...] + y + bout_ref[...]


def _gmlp(x, layer, gains, gl, wl, w_in, b_in, ln_g, ln_b, w_s, bs_full, w_out, b_out):
    t, d = x.shape
    half = w_out.shape[1]
    tm = GMLP_TILE
    assert tm % GMLP_CHUNK == 0 and t % tm == 0
    return pl.pallas_call(
        functools.partial(_gmlp_kernel, half=half),
        out_shape=jax.ShapeDtypeStruct((t, d), F32),
        grid=(t // tm,),
        in_specs=[
            pl.BlockSpec((tm, d), lambda i: (i, 0)),
            _layer_spec(layer, (1, d)),
            _layer_spec(wl, (d, 2 * half)),
            _layer_spec(gl, (1, 2 * half)),
            _layer_spec(gl, (1, half)),
            _layer_spec(gl, (1, half)),
            _layer_spec(gl, (GMLP_GROUPS, GMLP_CHUNK, GMLP_CHUNK)),
            _layer_spec(gl, (GMLP_CHUNK, half)),
            _layer_spec(wl, (half, d)),
            _layer_spec(gl, (1, d)),
        ],
        out_specs=pl.BlockSpec((tm, d), lambda i: (i, 0)),
        scratch_shapes=[pltpu.VMEM((tm, half), F32), pltpu.VMEM((tm, half), BF16)],
        compiler_params=_params(1),
        name="gmlp",
    )(x, gains, w_in, b_in, ln_g, ln_b, w_s, bs_full, w_out, b_out)


def _lambda_init(layer_idx):
    return 0.8 - 0.6 * math.exp(-0.3 * layer_idx)


def _rows(v):
    return v.reshape(v.shape[0], 1, v.shape[1]).astype(F32)


def kernel(x, ffn1_norm, ffn1_w_gate_up, ffn1_w_down, mix_norm, ffn2_norm, ffn2_w_gate_up, ffn2_w_down, attn_w_in, attn_w_out, attn_q_norm, attn_k_norm, attn_lambda_q1, attn_lambda_k1, attn_lambda_q2, attn_lambda_k2, attn_subln, gmlp_w_in, gmlp_b_in, gmlp_ln_g, gmlp_ln_b, gmlp_w_s, gmlp_b_s, gmlp_w_out, gmlp_b_out):
    b, s, d = x.shape
    depth = ffn1_norm.shape[0]
    h = DIFF_HEADS
    hd = d // h // 2
    nq = s // ATTN_TILE
    xf = x.reshape(b * s, d)

    ffn_gains = (_rows(ffn1_norm), _rows(ffn2_norm))
    ffn_f32 = ((ffn1_w_gate_up, ffn1_w_down), (ffn2_w_gate_up, ffn2_w_down))
    mix_gains = _rows(mix_norm)
    a_w_q_t = jnp.swapaxes(attn_w_in[:, :, :d], 1, 2).astype(BF16)
    a_w_k = attn_w_in[:, :, d:2 * d].astype(BF16)
    a_w_v_t = jnp.swapaxes(attn_w_in[:, :, 2 * d:], 1, 2).astype(BF16)
    a_w_out = attn_w_out.astype(BF16)
    feat_gain = attn_q_norm.astype(F32) * attn_k_norm.astype(F32) * (hd ** -0.5 * LOG2_E)
    kgain = _rows(jnp.tile(feat_gain, (1, 2 * h)))
    score_bound = 1.01 * hd * jnp.max(jnp.abs(feat_gain), axis=1)
    stable = (score_bound > SCORE_BOUND_LOG2).astype(jnp.int32)
    lam_vecs = [_rows(v) for v in (attn_lambda_q1, attn_lambda_k1, attn_lambda_q2, attn_lambda_k2)]
    subln_col = attn_subln.reshape(attn_subln.shape[0], -1, 1).astype(F32)
    half = gmlp_w_out.shape[1]
    g_vecs = (_rows(gmlp_b_in), _rows(gmlp_ln_g), _rows(gmlp_ln_b), gmlp_w_s.astype(F32),
              jnp.repeat(jnp.swapaxes(gmlp_b_s, 1, 2), half // GMLP_GROUPS, axis=2).astype(F32))

    w_bf16 = [ffn1_w_gate_up[0:1].astype(BF16), ffn1_w_down[0:1].astype(BF16)]

    def ffn(xf, which, i, proj=None, extra=()):
        nxt = (1, i) if which == 0 else (0, i + 1)
        cast = tuple((w, nxt[1]) for w in ffn_f32[nxt[0]]) if nxt[1] < depth else ()
        xf, *w_next = _ffn(xf, i, ffn_gains[which], *w_bf16, proj=proj, cast=cast + tuple(extra), w_layer=0)
        w_bf16[:] = w_next[:len(cast)]
        return xf, w_next[len(cast):]

    for i in range(depth):
        j = i // 2
        if i % 2 == 0:
            xf, _ = ffn(xf, 0, i)
            k, qt, vt = _attn_in(xf, i, mix_gains, j, a_w_k, a_w_q_t, a_w_v_t, kgain, hd)
            o = _flash(stable[j:j + 1], j, qt.reshape(b, nq, d, ATTN_TILE), k.reshape(b, s, d),
                       vt.reshape(b, nq, d, ATTN_TILE), *lam_vecs, subln_col, _lambda_init(i))
            proj = (o.reshape(b * s, d), a_w_out, j)
        else:
            xf, (g_w_in, g_w_out) = ffn(xf, 0, i, extra=((gmlp_w_in, j), (gmlp_w_out, j)))
            xf = _gmlp(xf, i, mix_gains, j, 0, g_w_in, *g_vecs, g_w_out, _rows(gmlp_b_out))
            proj = None
        xf, _ = ffn(xf, 1, i, proj=proj)
    return xf.reshape(b, s, d)
```

```python
import functools
import math

import jax
import jax.numpy as jnp
from jax import lax
from jax.experimental import pallas as pl
from jax.experimental.pallas import tpu as pltpu

F32 = jnp.float32
BF16 = jnp.bfloat16

RMS_EPS = 1e-6
LN_EPS = 1e-5
DIFF_HEADS = 8
GMLP_GROUPS = 8
GMLP_CHUNK = 128

MXU_TILE = 256
VMEM_LIMIT_BYTES = 56 * 1024 * 1024
NEG_BIG = -1e30
LOG2_E = math.log2(math.e)
SCORE_BOUND_LOG2 = 32.0

GMLP_TILE = 1024
FFN_TILE = 1024
ATTN_TILE = 512
ATTN_IN_TILES = 2
Q_TILES = 4
KV_BLOCKS_PER_STEP = 4


def _layer_spec(layer, shape, block_idx=None):
    block_idx = tuple(block_idx) if block_idx is not None else (0,) * len(shape)
    return pl.BlockSpec((None,) + tuple(shape), lambda *_: (layer,) + block_idx, pipeline_mode=pl.Buffered(1))


def _params(n_axes):
    return pltpu.CompilerParams(
        dimension_semantics=("arbitrary",) * n_axes,
        vmem_limit_bytes=VMEM_LIMIT_BYTES,
    )


def _rms_rows(x, gain):
    ms = jnp.mean(x * x, axis=-1, keepdims=True)
    return x * lax.rsqrt(ms + RMS_EPS) * gain


def _ffn_kernel(*refs, d_ff, with_proj, n_cast):
    refs = list(refs)
    x_ref = refs.pop(0)
    a_ref, wp_ref = (refs.pop(0), refs.pop(0)) if with_proj else (None, None)
    g_ref, wgu_ref, wd_ref = refs.pop(0), refs.pop(0), refs.pop(0)
    cast_src = [refs.pop(0) for _ in range(n_cast)]
    o_ref = refs.pop(0)
    cast_dst = [refs.pop(0) for _ in range(n_cast)]
    (h_ref,) = refs
    for src, dst in zip(cast_src, cast_dst):
        dst[...] = src[...].astype(BF16)
    if with_proj:
        o_ref[...] = x_ref[...] + jnp.dot(a_ref[...], wp_ref[...], preferred_element_type=F32)
        res_ref = o_ref
    else:
        res_ref = x_ref
    xn = _rms_rows(res_ref[...], g_ref[...]).astype(BF16)
    for j in range(d_ff // MXU_TILE):
        lo = j * MXU_TILE
        gate = jnp.dot(xn, wgu_ref[:, lo:lo + MXU_TILE], preferred_element_type=F32)
        up = jnp.dot(xn, wgu_ref[:, d_ff + lo:d_ff + lo + MXU_TILE], preferred_element_type=F32)
        h_ref[:, lo:lo + MXU_TILE] = (gate * jax.nn.sigmoid(gate) * up).astype(BF16)
    y = jnp.dot(h_ref[...], wd_ref[...], preferred_element_type=F32)
    o_ref[...] = res_ref[...] + 0.5 * y


def _ffn(x, layer, gains, w_gate_up, w_down, proj=None, cast=(), w_layer=0):
    t, d = x.shape
    d_ff = w_down.shape[1]
    assert d_ff % MXU_TILE == 0 and t % FFN_TILE == 0
    tm = FFN_TILE
    steps = t // tm
    row_spec = pl.BlockSpec((tm, d), lambda i: (i, 0))
    operands, in_specs = [x], [row_spec]
    if proj is not None:
        a, w_proj, proj_layer = proj
        operands += [a, w_proj]
        in_specs += [pl.BlockSpec((tm, a.shape[1]), lambda i: (i, 0)), _layer_spec(proj_layer, w_proj.shape[1:])]
    operands += [gains, w_gate_up, w_down]
    in_specs += [_layer_spec(layer, (1, d)), _layer_spec(w_layer, (d, 2 * d_ff)), _layer_spec(w_layer, (d_ff, d))]
    out_shape, out_specs = [jax.ShapeDtypeStruct((t, d), F32)], [row_spec]
    for stack, src_layer in cast:
        n_layers, rows, cols = stack.shape
        assert rows % steps == 0
        chunk = rows // steps
        operands.append(stack.reshape(n_layers, steps, chunk, cols))
        in_specs.append(pl.BlockSpec((None, None, chunk, cols), lambda i, src_layer=src_layer: (src_layer, i, 0, 0)))
        out_shape.append(jax.ShapeDtypeStruct((steps, chunk, cols), BF16))
        out_specs.append(pl.BlockSpec((None, chunk, cols), lambda i: (i, 0, 0)))
    outs = pl.pallas_call(
        functools.partial(_ffn_kernel, d_ff=d_ff, with_proj=proj is not None, n_cast=len(cast)),
        out_shape=out_shape,
        grid=(steps,),
        in_specs=in_specs,
        out_specs=out_specs,
        scratch_shapes=[pltpu.VMEM((tm, d_ff), BF16)],
        compiler_params=_params(1),
        name="ffn",
    )(*operands)
    return [outs[0]] + [o.reshape(1, stack.shape[1], stack.shape[2]) for o, (stack, _) in zip(outs[1:], cast)]


def _group_ones(n, group):
    r = lax.broadcasted_iota(jnp.int32, (n, n), 0) // group
    c = lax.broadcasted_iota(jnp.int32, (n, n), 1) // group
    return (r == c).astype(BF16)


def _attn_in_kernel(x_ref, g_ref, wk_ref, wqt_ref, wvt_ref, kgain_ref, k_ref, qt_ref, vt_ref, *, hd):
    d = x_ref.shape[1]
    xn = _rms_rows(x_ref[...], g_ref[...]).astype(BF16)
    ones = _group_ones(MXU_TILE, hd)
    nt_dims = (((1,), (1,)), ((), ()))

    k = jnp.dot(xn, wk_ref[...], preferred_element_type=F32)
    for j in range(d // MXU_TILE):
        lo = j * MXU_TILE
        kj = k[:, lo:lo + MXU_TILE]
        ss = jnp.dot((kj * kj).astype(BF16), ones, preferred_element_type=F32)
        kn = kj * lax.rsqrt(ss * (1.0 / hd) + RMS_EPS) * kgain_ref[:, lo:lo + MXU_TILE]
        k_ref[:, lo:lo + MXU_TILE] = kn.astype(BF16)

    qt = lax.dot_general(wqt_ref[...], xn, nt_dims, preferred_element_type=F32)
    qg = qt.reshape(d // hd, hd, qt.shape[1])
    ms = jnp.mean(qg * qg, axis=1, keepdims=True)
    qn = (qg * lax.rsqrt(ms + RMS_EPS)).reshape(qt.shape).astype(BF16)

    vt = lax.dot_general(wvt_ref[...], xn, nt_dims, preferred_element_type=F32).astype(BF16)
    tk = qt_ref.shape[2]
    for n in range(qt_ref.shape[0]):
        qt_ref[n] = qn[:, n * tk:(n + 1) * tk]
        vt_ref[n] = vt[:, n * tk:(n + 1) * tk]


def _attn_in(x, layer, gains, attn_layer, w_k, w_q_t, w_v_t, kgain, hd):
    t, d = x.shape
    per_step = ATTN_IN_TILES
    tm = per_step * ATTN_TILE
    nt = t // ATTN_TILE
    tile_major = pl.BlockSpec((per_step, d, ATTN_TILE), lambda i: (i, 0, 0))
    return pl.pallas_call(
        functools.partial(_attn_in_kernel, hd=hd),
        out_shape=(
            jax.ShapeDtypeStruct((t, d), BF16),
            jax.ShapeDtypeStruct((nt, d, ATTN_TILE), BF16),
            jax.ShapeDtypeStruct((nt, d, ATTN_TILE), BF16),
        ),
        grid=(t // tm,),
        in_specs=[
            pl.BlockSpec((tm, d), lambda i: (i, 0)),
            _layer_spec(layer, (1, d)),
            _layer_spec(attn_layer, (d, d)),
            _layer_spec(attn_layer, (d, d)),
            _layer_spec(attn_layer, (d, d)),
            _layer_spec(attn_layer, (1, d)),
        ],
        out_specs=(pl.BlockSpec((tm, d), lambda i: (i, 0)), tile_major, tile_major),
        compiler_params=_params(1),
        name="attn_in",
    )(x, gains, w_k, w_q_t, w_v_t, kgain)


def _flash_kernel(stable_ref, qt_ref, k_ref, vt_ref, lq1_ref, lk1_ref, lq2_ref, lk2_ref, sub_ref, o_ref,
                  q_scr, m_scr, l_scr, acc_scr, *, hd, lam_init):
    i = pl.program_id(2)
    tk = vt_ref.shape[3]
    tq = Q_TILES * tk

    qt = jnp.concatenate([qt_ref[0, n] for n in range(Q_TILES)], axis=1)
    row = lax.broadcasted_iota(jnp.int32, qt.shape, 0)
    zero = jnp.zeros_like(qt)
    q_scr[0] = jnp.where(row < hd, qt, zero)
    q_scr[1] = jnp.where(row >= hd, qt, zero)
    m_scr[...] = jnp.full(m_scr.shape, NEG_BIG, F32)
    l_scr[...] = jnp.zeros(l_scr.shape, F32)
    acc_scr[...] = jnp.zeros(acc_scr.shape, F32)

    def kv_block(j, nblk, q_lo, diagonal, stable):
        start = pl.multiple_of(j * tk, tk)
        kj = k_ref[0, pl.ds(start, nblk * tk), :]
        vj = jnp.concatenate([vt_ref[0, j + n] for n in range(nblk)], axis=1)
        for c in range(2):
            s = jnp.dot(kj, q_scr[c, :, q_lo:], preferred_element_type=F32)
            if diagonal:
                kpos = lax.broadcasted_iota(jnp.int32, s.shape, 0) - (nblk - 1) * tk
                qpos = lax.broadcasted_iota(jnp.int32, s.shape, 1)
                s = jnp.where(kpos <= qpos, s, NEG_BIG)
            if stable:
                m_old = m_scr[c, :, q_lo:]
                m_new = jnp.maximum(m_old, jnp.max(s, axis=0, keepdims=True))
                alpha = jnp.exp2(m_old - m_new)
                p = jnp.exp2(s - m_new)
                l_scr[c, :, q_lo:] = alpha * l_scr[c, :, q_lo:] + jnp.sum(p, axis=0, keepdims=True)
                acc_scr[c, :, q_lo:] = (alpha * acc_scr[c, :, q_lo:]
                                        + jnp.dot(vj, p.astype(BF16), preferred_element_type=F32))
                m_scr[c, :, q_lo:] = m_new
            else:
                p = jnp.exp2(s)
                l_scr[c, :, q_lo:] = l_scr[c, :, q_lo:] + jnp.sum(p, axis=0, keepdims=True)
                acc_scr[c, :, q_lo:] = acc_scr[c, :, q_lo:] + jnp.dot(vj, p.astype(BF16), preferred_element_type=F32)

    def run(stable, chunk):
        n_below = Q_TILES * i
        n_trips = n_below // chunk

        def body(jj, carry):
            kv_block(jj * chunk, chunk, 0, diagonal=False, stable=stable)
            return carry
        lax.fori_loop(0, n_trips, body, 0)

        for rem in range(0, chunk, Q_TILES):
            @pl.when(n_below - n_trips * chunk == rem)
            def _():
                kv_block(n_trips * chunk, rem + 1, 0, diagonal=True, stable=stable)
                for n in range(1, Q_TILES):
                    kv_block(n_below + n, 1, n * tk, diagonal=True, stable=stable)

    @pl.when(stable_ref[0] == 0)
    def _():
        run(stable=False, chunk=KV_BLOCKS_PER_STEP)

    @pl.when(stable_ref[0] != 0)
    def _():
        run(stable=True, chunk=1)

    lam = (jnp.exp(jnp.sum(lq1_ref[...] * lk1_ref[...], axis=-1, keepdims=True))
           - jnp.exp(jnp.sum(lq2_ref[...] * lk2_ref[...], axis=-1, keepdims=True))
           + lam_init)
    o = acc_scr[0] * (1.0 / l_scr[0]) - acc_scr[1] * (lam / l_scr[1])
    ms = jnp.mean(o * o, axis=0, keepdims=True)
    on = o * lax.rsqrt(ms + RMS_EPS) * (sub_ref[...] * (1.0 - lam_init))
    o_ref[0] = on.T.astype(BF16)


def _flash(stable, attn_layer, qt, k, vt, lq1, lk1, lq2, lk2, subln_col, lam_init):
    b, nk, d, tk = vt.shape
    _, s, _ = k.shape
    assert qt.shape == vt.shape and nk % Q_TILES == 0 and KV_BLOCKS_PER_STEP % Q_TILES == 0
    nq = nk // Q_TILES
    tq = Q_TILES * tk
    h = DIFF_HEADS
    hw = d // h
    hd = hw // 2
    vec = pl.BlockSpec((None, 1, hd), lambda bb, hh, ii, st: (attn_layer, 0, 0))
    grid_spec = pltpu.PrefetchScalarGridSpec(
        num_scalar_prefetch=1,
        grid=(b, h, nq),
        in_specs=[
            pl.BlockSpec((1, Q_TILES, hw, tk), lambda bb, hh, ii, st: (bb, ii, hh, 0)),
            pl.BlockSpec((1, s, hw), lambda bb, hh, ii, st: (bb, 0, hh)),
            pl.BlockSpec((1, nk, hw, tk), lambda bb, hh, ii, st: (bb, 0, hh, 0)),
            vec, vec, vec, vec,
            pl.BlockSpec((None, hw, 1), lambda bb, hh, ii, st: (attn_layer, 0, 0)),
        ],
        out_specs=pl.BlockSpec((1, tq, hw), lambda bb, hh, ii, st: (bb, ii, hh)),
        scratch_shapes=[
            pltpu.VMEM((2, hw, tq), BF16),
            pltpu.VMEM((2, 1, tq), F32),
            pltpu.VMEM((2, 1, tq), F32),
            pltpu.VMEM((2, hw, tq), F32),
        ],
    )
    return pl.pallas_call(
        functools.partial(_flash_kernel, hd=hd, lam_init=lam_init),
        out_shape=jax.ShapeDtypeStruct((b, s, d), BF16),
        grid_spec=grid_spec,
        compiler_params=_params(3),
        name="flash",
    )(stable, qt, k, vt, lq1, lk1, lq2, lk2, subln_col)


def _gelu(z):
    return 0.5 * z * (1.0 + lax.erf(z * math.sqrt(0.5)))


def _gmlp_kernel(x_ref, g_ref, win_ref, bin_ref, lng_ref, lnb_ref, ws_ref, bs_ref, wout_ref, bout_ref,
                 y_ref, v_scr, gated_scr, *, half):
    tm = x_ref.shape[0]
    gw = half // GMLP_GROUPS
    xn = _rms_rows(x_ref[...], g_ref[...]).astype(BF16)

    for g in range(GMLP_GROUPS):
        lo = half + g * gw
        z = jnp.dot(xn, win_ref[:, lo:lo + gw], preferred_element_type=F32) + bin_ref[:, lo:lo + gw]
        v = _gelu(z)
        v_scr[:, g * gw:(g + 1) * gw] = v
        if g == 0:
            c = jnp.sum(v, axis=-1, keepdims=True) * (1.0 / gw)
            rsum = jnp.zeros((tm, 1), F32)
            rsq = jnp.zeros((tm, 1), F32)
        dv = v - c
        rsum = rsum + jnp.sum(dv, axis=-1, keepdims=True)
        rsq = rsq + jnp.sum(dv * dv, axis=-1, keepdims=True)
    dmean = rsum * (1.0 / half)
    mu = c + dmean
    rstd = lax.rsqrt(rsq * (1.0 / half) - dmean * dmean + LN_EPS)

    t_idx = lax.broadcasted_iota(jnp.int32, (GMLP_CHUNK, GMLP_CHUNK), 0)
    s_idx = lax.broadcasted_iota(jnp.int32, (GMLP_CHUNK, GMLP_CHUNK), 1)
    causal = s_idx <= t_idx

    for g in range(GMLP_GROUPS):
        lo = g * gw
        vn = ((v_scr[:, lo:lo + gw] - mu) * rstd * lng_ref[:, lo:lo + gw] + lnb_ref[:, lo:lo + gw]).astype(BF16)
        w = jnp.where(causal, ws_ref[g], 0.0).astype(BF16)
        z = jnp.dot(xn, win_ref[:, lo:lo + gw], preferred_element_type=F32) + bin_ref[:, lo:lo + gw]
        u = _gelu(z)
        for c in range(tm // GMLP_CHUNK):
            r = c * GMLP_CHUNK
            s = jnp.dot(w, vn[r:r + GMLP_CHUNK, :], preferred_element_type=F32) + bs_ref[:, lo:lo + gw]
            gated_scr[r:r + GMLP_CHUNK, lo:lo + gw] = (u[r:r + GMLP_CHUNK, :] * s).astype(BF16)

    y = jnp.dot(gated_scr[...], wout_ref[...], preferred_element_type=F32)
    y_ref[...] = x_ref[...] + y + bout_ref[...]


def _gmlp(x, layer, gains, gl, wl, w_in, b_in, ln_g, ln_b, w_s, bs_full, w_out, b_out):
    t, d = x.shape
    half = w_out.shape[1]
    tm = GMLP_TILE
    assert tm % GMLP_CHUNK == 0 and t % tm == 0
    return pl.pallas_call(
        functools.partial(_gmlp_kernel, half=half),
        out_shape=jax.ShapeDtypeStruct((t, d), F32),
        grid=(t // tm,),
        in_specs=[
            pl.BlockSpec((tm, d), lambda i: (i, 0)),
            _layer_spec(layer, (1, d)),
            _layer_spec(wl, (d, 2 * half)),
            _layer_spec(gl, (1, 2 * half)),
            _layer_spec(gl, (1, half)),
            _layer_spec(gl, (1, half)),
            _layer_spec(gl, (GMLP_GROUPS, GMLP_CHUNK, GMLP_CHUNK)),
            _layer_spec(gl, (GMLP_CHUNK, half)),
            _layer_spec(wl, (half, d)),
            _layer_spec(gl, (1, d)),
        ],
        out_specs=pl.BlockSpec((tm, d), lambda i: (i, 0)),
        scratch_shapes=[pltpu.VMEM((tm, half), F32), pltpu.VMEM((tm, half), BF16)],
        compiler_params=_params(1),
        name="gmlp",
    )(x, gains, w_in, b_in, ln_g, ln_b, w_s, bs_full, w_out, b_out)


def _lambda_init(layer_idx):
    return 0.8 - 0.6 * math.exp(-0.3 * layer_idx)


def _rows(v):
    return v.reshape(v.shape[0], 1, v.shape[1]).astype(F32)


def kernel(x, ffn1_norm, ffn1_w_gate_up, ffn1_w_down, mix_norm, ffn2_norm, ffn2_w_gate_up, ffn2_w_down, attn_w_in, attn_w_out, attn_q_norm, attn_k_norm, attn_lambda_q1, attn_lambda_k1, attn_lambda_q2, attn_lambda_k2, attn_subln, gmlp_w_in, gmlp_b_in, gmlp_ln_g, gmlp_ln_b, gmlp_w_s, gmlp_b_s, gmlp_w_out, gmlp_b_out):
    b, s, d = x.shape
    depth = ffn1_norm.shape[0]
    h = DIFF_HEADS
    hd = d // h // 2
    nq = s // ATTN_TILE
    xf = x.reshape(b * s, d)

    ffn_gains = (_rows(ffn1_norm), _rows(ffn2_norm))
    ffn_f32 = ((ffn1_w_gate_up, ffn1_w_down), (ffn2_w_gate_up, ffn2_w_down))
    mix_gains = _rows(mix_norm)
    a_w_q_t = jnp.swapaxes(attn_w_in[:, :, :d], 1, 2).astype(BF16)
    a_w_k = attn_w_in[:, :, d:2 * d].astype(BF16)
    a_w_v_t = jnp.swapaxes(attn_w_in[:, :, 2 * d:], 1, 2).astype(BF16)
    a_w_out = attn_w_out.astype(BF16)
    feat_gain = attn_q_norm.astype(F32) * attn_k_norm.astype(F32) * (hd ** -0.5 * LOG2_E)
    kgain = _rows(jnp.tile(feat_gain, (1, 2 * h)))
    score_bound = 1.01 * hd * jnp.max(jnp.abs(feat_gain), axis=1)
    stable = (score_bound > SCORE_BOUND_LOG2).astype(jnp.int32)
    lam_vecs = [_rows(v) for v in (attn_lambda_q1, attn_lambda_k1, attn_lambda_q2, attn_lambda_k2)]
    subln_col = attn_subln.reshape(attn_subln.shape[0], -1, 1).astype(F32)
    half = gmlp_w_out.shape[1]
    g_vecs = (_rows(gmlp_b_in), _rows(gmlp_ln_g), _rows(gmlp_ln_b), gmlp_w_s.astype(F32),
              jnp.repeat(jnp.swapaxes(gmlp_b_s, 1, 2), half // GMLP_GROUPS, axis=2).astype(F32))

    w_bf16 = [ffn1_w_gate_up[0:1].astype(BF16), ffn1_w_down[0:1].astype(BF16)]

    def ffn(xf, which, i, proj=None, extra=()):
        nxt = (1, i) if which == 0 else (0, i + 1)
        cast = tuple((w, nxt[1]) for w in ffn_f32[nxt[0]]) if nxt[1] < depth else ()
        xf, *w_next = _ffn(xf, i, ffn_gains[which], *w_bf16, proj=proj, cast=cast + tuple(extra), w_layer=0)
        w_bf16[:] = w_next[:len(cast)]
        return xf, w_next[len(cast):]

    for i in range(depth):
        j = i // 2
        if i % 2 == 0:
            xf, _ = ffn(xf, 0, i)
            k, qt, vt = _attn_in(xf, i, mix_gains, j, a_w_k, a_w_q_t, a_w_v_t, kgain, hd)
            o = _flash(stable[j:j + 1], j, qt.reshape(b, nq, d, ATTN_TILE), k.reshape(b, s, d),
                       vt.reshape(b, nq, d, ATTN_TILE), *lam_vecs, subln_col, _lambda_init(i))
            proj = (o.reshape(b * s, d), a_w_out, j)
        else:
            xf, (g_w_in, g_w_out) = ffn(xf, 0, i, extra=((gmlp_w_in, j), (gmlp_w_out, j)))
            xf = _gmlp(xf, i, mix_gains, j, 0, g_w_in, *g_vecs, g_w_out, _rows(gmlp_b_out))
            proj = None
        xf, _ = ffn(xf, 1, i, proj=proj)
    return xf.reshape(b, s, d)
```

```python
import functools
import math

import jax
import jax.numpy as jnp
from jax import lax
from jax.experimental import pallas as pl
from jax.experimental.pallas import tpu as pltpu

F32 = jnp.float32
BF16 = jnp.bfloat16

RMS_EPS = 1e-6
LN_EPS = 1e-5
DIFF_HEADS = 8
GMLP_GROUPS = 8
GMLP_CHUNK = 128

MXU_TILE = 256
VMEM_LIMIT_BYTES = 56 * 1024 * 1024
NEG_BIG = -1e30
LOG2_E = math.log2(math.e)
SCORE_BOUND_LOG2 = 32.0

GMLP_TILE = 1024
FFN_TILE = 1024
ATTN_TILE = 512
ATTN_IN_TILES = 2
Q_TILES = 4
KV_BLOCKS_PER_STEP = 4


def _layer_spec(layer, shape, block_idx=None):
    block_idx = tuple(block_idx) if block_idx is not None else (0,) * len(shape)
    return pl.BlockSpec((None,) + tuple(shape), lambda *_: (layer,) + block_idx, pipeline_mode=pl.Buffered(1))


def _params(n_axes):
    return pltpu.CompilerParams(
        dimension_semantics=("arbitrary",) * n_axes,
        vmem_limit_bytes=VMEM_LIMIT_BYTES,
    )


def _rms_rows(x, gain):
    ms = jnp.mean(x * x, axis=-1, keepdims=True)
    return x * lax.rsqrt(ms + RMS_EPS) * gain


def _ffn_kernel(*refs, d_ff, with_proj, n_cast):
    refs = list(refs)
    x_ref = refs.pop(0)
    a_ref, wp_ref = (refs.pop(0), refs.pop(0)) if with_proj else (None, None)
    g_ref, wgu_ref, wd_ref = refs.pop(0), refs.pop(0), refs.pop(0)
    cast_src = [refs.pop(0) for _ in range(n_cast)]
    o_ref = refs.pop(0)
    cast_dst = [refs.pop(0) for _ in range(n_cast)]
    (h_ref,) = refs
    for src, dst in zip(cast_src, cast_dst):
        dst[...] = src[...].astype(BF16)
    if with_proj:
        o_ref[...] = x_ref[...] + jnp.dot(a_ref[...], wp_ref[...], preferred_element_type=F32)
        res_ref = o_ref
    else:
        res_ref = x_ref
    xn = _rms_rows(res_ref[...], g_ref[...]).astype(BF16)
    for j in range(d_ff // MXU_TILE):
        lo = j * MXU_TILE
        gate = jnp.dot(xn, wgu_ref[:, lo:lo + MXU_TILE], preferred_element_type=F32)
        up = jnp.dot(xn, wgu_ref[:, d_ff + lo:d_ff + lo + MXU_TILE], preferred_element_type=F32)
        h_ref[:, lo:lo + MXU_TILE] = (gate * (0.5 + 0.5 * jnp.tanh(0.5 * gate)) * up).astype(BF16)
    y = jnp.dot(h_ref[...], wd_ref[...], preferred_element_type=F32)
    o_ref[...] = res_ref[...] + 0.5 * y


def _ffn(x, layer, gains, w_gate_up, w_down, proj=None, cast=(), w_layer=0):
    t, d = x.shape
    d_ff = w_down.shape[1]
    assert d_ff % MXU_TILE == 0 and t % FFN_TILE == 0
    tm = FFN_TILE
    steps = t // tm
    row_spec = pl.BlockSpec((tm, d), lambda i: (i, 0))
    operands, in_specs = [x], [row_spec]
    if proj is not None:
        a, w_proj, proj_layer = proj
        operands += [a, w_proj]
        in_specs += [pl.BlockSpec((tm, a.shape[1]), lambda i: (i, 0)), _layer_spec(proj_layer, w_proj.shape[1:])]
    operands += [gains, w_gate_up, w_down]
    in_specs += [_layer_spec(layer, (1, d)), _layer_spec(w_layer, (d, 2 * d_ff)), _layer_spec(w_layer, (d_ff, d))]
    out_shape, out_specs = [jax.ShapeDtypeStruct((t, d), F32)], [row_spec]
    for stack, src_layer in cast:
        n_layers, rows, cols = stack.shape
        assert rows % steps == 0
        chunk = rows // steps
        operands.append(stack.reshape(n_layers, steps, chunk, cols))
        in_specs.append(pl.BlockSpec((None, None, chunk, cols), lambda i, src_layer=src_layer: (src_layer, i, 0, 0)))
        out_shape.append(jax.ShapeDtypeStruct((steps, chunk, cols), BF16))
        out_specs.append(pl.BlockSpec((None, chunk, cols), lambda i: (i, 0, 0)))
    outs = pl.pallas_call(
        functools.partial(_ffn_kernel, d_ff=d_ff, with_proj=proj is not None, n_cast=len(cast)),
        out_shape=out_shape,
        grid=(steps,),
        in_specs=in_specs,
        out_specs=out_specs,
        scratch_shapes=[pltpu.VMEM((tm, d_ff), BF16)],
        compiler_params=_params(1),
        name="ffn",
    )(*operands)
    return [outs[0]] + [o.reshape(1, stack.shape[1], stack.shape[2]) for o, (stack, _) in zip(outs[1:], cast)]


def _attn_in_kernel(x_ref, g_ref, wkt_ref, wqt_ref, wvt_ref, kgain_ref, k_ref, qt_ref, vt_ref, *, hd):
    d = x_ref.shape[1]
    xn = _rms_rows(x_ref[...], g_ref[...]).astype(BF16)
    nt_dims = (((1,), (1,)), ((), ()))

    def project_normed(wt_ref):
        pt = lax.dot_general(wt_ref[...], xn, nt_dims, preferred_element_type=F32)
        pg = pt.reshape(d // hd, hd, pt.shape[1])
        ms = jnp.mean(pg * pg, axis=1, keepdims=True)
        return (pg * lax.rsqrt(ms + RMS_EPS)).reshape(pt.shape)

    k_ref[...] = (project_normed(wkt_ref).T * kgain_ref[...]).astype(BF16)
    qn = project_normed(wqt_ref).astype(BF16)

    vt = lax.dot_general(wvt_ref[...], xn, nt_dims, preferred_element_type=F32).astype(BF16)
    tk = qt_ref.shape[2]
    for n in range(qt_ref.shape[0]):
        qt_ref[n] = qn[:, n * tk:(n + 1) * tk]
        vt_ref[n] = vt[:, n * tk:(n + 1) * tk]


def _attn_in(x, layer, gains, attn_layer, w_in_t, kgain, hd):
    t, d = x.shape
    per_step = ATTN_IN_TILES
    tm = per_step * ATTN_TILE
    nt = t // ATTN_TILE
    tile_major = pl.BlockSpec((per_step, d, ATTN_TILE), lambda i: (i, 0, 0))
    return pl.pallas_call(
        functools.partial(_attn_in_kernel, hd=hd),
        out_shape=(
            jax.ShapeDtypeStruct((t, d), BF16),
            jax.ShapeDtypeStruct((nt, d, ATTN_TILE), BF16),
            jax.ShapeDtypeStruct((nt, d, ATTN_TILE), BF16),
        ),
        grid=(t // tm,),
        in_specs=[
            pl.BlockSpec((tm, d), lambda i: (i, 0)),
            _layer_spec(layer, (1, d)),
            _layer_spec(attn_layer, (d, d), (1, 0)),
            _layer_spec(attn_layer, (d, d), (0, 0)),
            _layer_spec(attn_layer, (d, d), (2, 0)),
            _layer_spec(attn_layer, (1, d)),
        ],
        out_specs=(pl.BlockSpec((tm, d), lambda i: (i, 0)), tile_major, tile_major),
        compiler_params=_params(1),
        name="attn_in",
    )(x, gains, w_in_t, w_in_t, w_in_t, kgain)


def _flash_kernel(stable_ref, qt_ref, k_ref, vt_ref, lq1_ref, lk1_ref, lq2_ref, lk2_ref, sub_ref, o_ref,
                  q_scr, m_scr, l_scr, acc_scr, *, hd, lam_init):
    i = pl.program_id(2)
    tk = vt_ref.shape[3]
    tq = Q_TILES * tk

    qt = jnp.concatenate([qt_ref[0, n] for n in range(Q_TILES)], axis=1)
    row = lax.broadcasted_iota(jnp.int32, qt.shape, 0)
    zero = jnp.zeros_like(qt)
    q_scr[0] = jnp.where(row < hd, qt, zero)
    q_scr[1] = jnp.where(row >= hd, qt, zero)
    m_scr[...] = jnp.full(m_scr.shape, NEG_BIG, F32)
    l_scr[...] = jnp.zeros(l_scr.shape, F32)
    acc_scr[...] = jnp.zeros(acc_scr.shape, F32)

    def kv_block(j, nblk, q_lo, diagonal, stable):
        start = pl.multiple_of(j * tk, tk)
        kj = k_ref[0, pl.ds(start, nblk * tk), :]
        vj = jnp.concatenate([vt_ref[0, j + n] for n in range(nblk)], axis=1)
        for c in range(2):
            s = jnp.dot(kj, q_scr[c, :, q_lo:], preferred_element_type=F32)
            if diagonal:
                kpos = lax.broadcasted_iota(jnp.int32, s.shape, 0) - (nblk - 1) * tk
                qpos = lax.broadcasted_iota(jnp.int32, s.shape, 1)
                s = jnp.where(kpos <= qpos, s, NEG_BIG)
            if stable:
                m_old = m_scr[c, :, q_lo:]
                m_new = jnp.maximum(m_old, jnp.max(s, axis=0, keepdims=True))
                alpha = jnp.exp2(m_old - m_new)
                p = jnp.exp2(s - m_new)
                l_scr[c, :, q_lo:] = alpha * l_scr[c, :, q_lo:] + jnp.sum(p, axis=0, keepdims=True)
                acc_scr[c, :, q_lo:] = (alpha * acc_scr[c, :, q_lo:]
                                        + jnp.dot(vj, p.astype(BF16), preferred_element_type=F32))
                m_scr[c, :, q_lo:] = m_new
            else:
                p = jnp.exp2(s)
                l_scr[c, :, q_lo:] = l_scr[c, :, q_lo:] + jnp.sum(p, axis=0, keepdims=True)
                acc_scr[c, :, q_lo:] = acc_scr[c, :, q_lo:] + jnp.dot(vj, p.astype(BF16), preferred_element_type=F32)

    def run(stable, chunk):
        n_below = Q_TILES * i
        n_trips = n_below // chunk

        def body(jj, carry):
            kv_block(jj * chunk, chunk, 0, diagonal=False, stable=stable)
            return carry
        lax.fori_loop(0, n_trips, body, 0)

        for rem in range(0, chunk, Q_TILES):
            @pl.when(n_below - n_trips * chunk == rem)
            def _():
                kv_block(n_trips * chunk, rem + 1, 0, diagonal=True, stable=stable)
                for n in range(1, Q_TILES):
                    kv_block(n_below + n, 1, n * tk, diagonal=True, stable=stable)

    @pl.when(stable_ref[0] == 0)
    def _():
        run(stable=False, chunk=KV_BLOCKS_PER_STEP)

    @pl.when(stable_ref[0] != 0)
    def _():
        run(stable=True, chunk=1)

    lam = (jnp.exp(jnp.sum(lq1_ref[...] * lk1_ref[...], axis=-1, keepdims=True))
           - jnp.exp(jnp.sum(lq2_ref[...] * lk2_ref[...], axis=-1, keepdims=True))
           + lam_init)
    o = acc_scr[0] * (1.0 / l_scr[0]) - acc_scr[1] * (lam / l_scr[1])
    ms = jnp.mean(o * o, axis=0, keepdims=True)
    on = o * lax.rsqrt(ms + RMS_EPS) * (sub_ref[...] * (1.0 - lam_init))
    o_ref[0] = on.T.astype(BF16)


def _flash(stable, attn_layer, qt, k, vt, lq1, lk1, lq2, lk2, subln_col, lam_init):
    b, nk, d, tk = vt.shape
    _, s, _ = k.shape
    assert qt.shape == vt.shape and nk % Q_TILES == 0 and KV_BLOCKS_PER_STEP % Q_TILES == 0
    nq = nk // Q_TILES
    tq = Q_TILES * tk
    h = DIFF_HEADS
    hw = d // h
    hd = hw // 2
    vec = pl.BlockSpec((None, 1, hd), lambda bb, hh, ii, st: (attn_layer, 0, 0))
    grid_spec = pltpu.PrefetchScalarGridSpec(
        num_scalar_prefetch=1,
        grid=(b, h, nq),
        in_specs=[
            pl.BlockSpec((1, Q_TILES, hw, tk), lambda bb, hh, ii, st: (bb, ii, hh, 0)),
            pl.BlockSpec((1, s, hw), lambda bb, hh, ii, st: (bb, 0, hh)),
            pl.BlockSpec((1, nk, hw, tk), lambda bb, hh, ii, st: (bb, 0, hh, 0)),
            vec, vec, vec, vec,
            pl.BlockSpec((None, hw, 1), lambda bb, hh, ii, st: (attn_layer, 0, 0)),
        ],
        out_specs=pl.BlockSpec((1, tq, hw), lambda bb, hh, ii, st: (bb, ii, hh)),
        scratch_shapes=[
            pltpu.VMEM((2, hw, tq), BF16),
            pltpu.VMEM((2, 1, tq), F32),
            pltpu.VMEM((2, 1, tq), F32),
            pltpu.VMEM((2, hw, tq), F32),
        ],
    )
    return pl.pallas_call(
        functools.partial(_flash_kernel, hd=hd, lam_init=lam_init),
        out_shape=jax.ShapeDtypeStruct((b, s, d), BF16),
        grid_spec=grid_spec,
        compiler_params=_params(3),
        name="flash",
    )(stable, qt, k, vt, lq1, lk1, lq2, lk2, subln_col)


def _gelu(z):
    return 0.5 * z * (1.0 + lax.erf(z * math.sqrt(0.5)))


def _gmlp_kernel(x_ref, g_ref, win_ref, bin_ref, lng_ref, lnb_ref, ws_ref, bs_ref, wout_ref, bout_ref,
                 y_ref, v_scr, gated_scr, *, half):
    tm = x_ref.shape[0]
    gw = half // GMLP_GROUPS
    xn = _rms_rows(x_ref[...], g_ref[...]).astype(BF16)

    for g in range(GMLP_GROUPS):
        lo = half + g * gw
        z = jnp.dot(xn, win_ref[:, lo:lo + gw], preferred_element_type=F32) + bin_ref[:, lo:lo + gw]
        v = _gelu(z)
        v_scr[:, g * gw:(g + 1) * gw] = v
        if g == 0:
            c = jnp.sum(v, axis=-1, keepdims=True) * (1.0 / gw)
            rsum = jnp.zeros((tm, 1), F32)
            rsq = jnp.zeros((tm, 1), F32)
        dv = v - c
        rsum = rsum + jnp.sum(dv, axis=-1, keepdims=True)
        rsq = rsq + jnp.sum(dv * dv, axis=-1, keepdims=True)
    dmean = rsum * (1.0 / half)
    mu = c + dmean
    rstd = lax.rsqrt(rsq * (1.0 / half) - dmean * dmean + LN_EPS)

    t_idx = lax.broadcasted_iota(jnp.int32, (GMLP_CHUNK, GMLP_CHUNK), 0)
    s_idx = lax.broadcasted_iota(jnp.int32, (GMLP_CHUNK, GMLP_CHUNK), 1)
    causal = s_idx <= t_idx

    for g in range(GMLP_GROUPS):
        lo = g * gw
        vn = ((v_scr[:, lo:lo + gw] - mu) * rstd * lng_ref[:, lo:lo + gw] + lnb_ref[:, lo:lo + gw]).astype(BF16)
        w = jnp.where(causal, ws_ref[g], 0.0).astype(BF16)
        z = jnp.dot(xn, win_ref[:, lo:lo + gw], preferred_element_type=F32) + bin_ref[:, lo:lo + gw]
        u = _gelu(z)
        for c in range(tm // GMLP_CHUNK):
            r = c * GMLP_CHUNK
            s = jnp.dot(w, vn[r:r + GMLP_CHUNK, :], preferred_element_type=F32) + bs_ref[:, lo:lo + gw]
            gated_scr[r:r + GMLP_CHUNK, lo:lo + gw] = (u[r:r + GMLP_CHUNK, :] * s).astype(BF16)

    y = jnp.dot(gated_scr[...], wout_ref[...], preferred_element_type=F32)
    y_ref[...] = x_ref[...] + y + bout_ref[...]


def _gmlp(x, layer, gains, gl, wl, w_in, b_in, ln_g, ln_b, w_s, bs_full, w_out, b_out):
    t, d = x.shape
    half = w_out.shape[1]
    tm = GMLP_TILE
    assert tm % GMLP_CHUNK == 0 and t % tm == 0
    return pl.pallas_call(
        functools.partial(_gmlp_kernel, half=half),
        out_shape=jax.ShapeDtypeStruct((t, d), F32),
        grid=(t // tm,),
        in_specs=[
            pl.BlockSpec((tm, d), lambda i: (i, 0)),
            _layer_spec(layer, (1, d)),
            _layer_spec(wl, (d, 2 * half)),
            _layer_spec(gl, (1, 2 * half)),
            _layer_spec(gl, (1, half)),
            _layer_spec(gl, (1, half)),
            _layer_spec(gl, (GMLP_GROUPS, GMLP_CHUNK, GMLP_CHUNK)),
            _layer_spec(gl, (GMLP_CHUNK, half)),
            _layer_spec(wl, (half, d)),
            _layer_spec(gl, (1, d)),
        ],
        out_specs=pl.BlockSpec((tm, d), lambda i: (i, 0)),
        scratch_shapes=[pltpu.VMEM((tm, half), F32), pltpu.VMEM((tm, half), BF16)],
        compiler_params=_params(1),
        name="gmlp",
    )(x, gains, w_in, b_in, ln_g, ln_b, w_s, bs_full, w_out, b_out)


def _lambda_init(layer_idx):
    return 0.8 - 0.6 * math.exp(-0.3 * layer_idx)


def _rows(v):
    return v.reshape(v.shape[0], 1, v.shape[1]).astype(F32)


def kernel(x, ffn1_norm, ffn1_w_gate_up, ffn1_w_down, mix_norm, ffn2_norm, ffn2_w_gate_up, ffn2_w_down, attn_w_in, attn_w_out, attn_q_norm, attn_k_norm, attn_lambda_q1, attn_lambda_k1, attn_lambda_q2, attn_lambda_k2, attn_subln, gmlp_w_in, gmlp_b_in, gmlp_ln_g, gmlp_ln_b, gmlp_w_s, gmlp_b_s, gmlp_w_out, gmlp_b_out):
    b, s, d = x.shape
    depth = ffn1_norm.shape[0]
    h = DIFF_HEADS
    hd = d // h // 2
    nq = s // ATTN_TILE
    xf = x.reshape(b * s, d)

    ffn_gains = (_rows(ffn1_norm), _rows(ffn2_norm))
    ffn_f32 = ((ffn1_w_gate_up, ffn1_w_down), (ffn2_w_gate_up, ffn2_w_down))
    mix_gains = _rows(mix_norm)
    a_w_in_t = jnp.swapaxes(attn_w_in, 1, 2).astype(BF16)
    a_w_out = attn_w_out.astype(BF16)
    feat_gain = attn_q_norm.astype(F32) * attn_k_norm.astype(F32) * (hd ** -0.5 * LOG2_E)
    kgain = _rows(jnp.tile(feat_gain, (1, 2 * h)))
    score_bound = 1.01 * hd * jnp.max(jnp.abs(feat_gain), axis=1)
    stable = (score_bound > SCORE_BOUND_LOG2).astype(jnp.int32)
    lam_vecs = [_rows(v) for v in (attn_lambda_q1, attn_lambda_k1, attn_lambda_q2, attn_lambda_k2)]
    subln_col = attn_subln.reshape(attn_subln.shape[0], -1, 1).astype(F32)
    half = gmlp_w_out.shape[1]
    g_vecs = (_rows(gmlp_b_in), _rows(gmlp_ln_g), _rows(gmlp_ln_b), gmlp_w_s.astype(F32),
              jnp.repeat(jnp.swapaxes(gmlp_b_s, 1, 2), half // GMLP_GROUPS, axis=2).astype(F32))

    w_bf16 = [ffn1_w_gate_up[0:1].astype(BF16), ffn1_w_down[0:1].astype(BF16)]

    def ffn(xf, which, i, proj=None, extra=()):
        nxt = (1, i) if which == 0 else (0, i + 1)
        cast = tuple((w, nxt[1]) for w in ffn_f32[nxt[0]]) if nxt[1] < depth else ()
        xf, *w_next = _ffn(xf, i, ffn_gains[which], *w_bf16, proj=proj, cast=cast + tuple(extra), w_layer=0)
        w_bf16[:] = w_next[:len(cast)]
        return xf, w_next[len(cast):]

    for i in range(depth):
        j = i // 2
        if i % 2 == 0:
            xf, _ = ffn(xf, 0, i)
            k, qt, vt = _attn_in(xf, i, mix_gains, j, a_w_in_t, kgain, hd)
            o = _flash(stable[j:j + 1], j, qt.reshape(b, nq, d, ATTN_TILE), k.reshape(b, s, d),
                       vt.reshape(b, nq, d, ATTN_TILE), *lam_vecs, subln_col, _lambda_init(i))
            proj = (o.reshape(b * s, d), a_w_out, j)
        else:
            xf, (g_w_in, g_w_out) = ffn(xf, 0, i, extra=((gmlp_w_in, j), (gmlp_w_out, j)))
            xf = _gmlp(xf, i, mix_gains, j, 0, g_w_in, *g_vecs, g_w_out, _rows(gmlp_b_out))
            proj = None
        xf, _ = ffn(xf, 1, i, proj=proj)
    return xf.reshape(b, s, d)
```
